```python
import math
import jax
import jax.numpy as jnp
from jax import lax
import numpy as np

D_MODEL = 1024
BATCH = 8
SEQ = 2048
DEPTH = 2
DEC_BATCH = 128
DEC_SEQ = 1
PAST_LEN = 2048
PAGE_SIZE = 128

N_HEADS = 16
N_KV_HEADS = 8
HEAD_DIM = 64
ATT_WIDTH = N_HEADS * HEAD_DIM
KV_WIDTH = N_KV_HEADS * HEAD_DIM
Q_BLOCK = 128
FORGET_BIAS_INIT = 3.0
SSM_EXPAND = 2
D_INNER = SSM_EXPAND * D_MODEL
SSM_HEAD_DIM = 64
SSM_HEADS = D_INNER // SSM_HEAD_DIM
SSM_GROUPS = 4
D_STATE = 128
CONV_WIDTH = 4
CONV_DIM = D_INNER + 2 * SSM_GROUPS * D_STATE
SSD_CHUNK = 128
N_EXPERTS = 16
N_EXPERT_GROUPS = 4
EXPERTS_PER_GROUP = N_EXPERTS // N_EXPERT_GROUPS
TOP_K = 2
D_FF_EXPERT = D_MODEL // 2
IN_SIZES = (ATT_WIDTH, KV_WIDTH, KV_WIDTH, N_HEADS, D_INNER, CONV_DIM, SSM_HEADS, D_MODEL, D_MODEL)
IN_DIM = ATT_WIDTH + 2 * KV_WIDTH + N_HEADS + D_INNER + CONV_DIM + SSM_HEADS + 2 * D_MODEL
EPS = 1e-6

kernel_name = 'fox_ssd_gated_moe_adaln_decode_step'


def _block(n, size):
    return size if n % size == 0 else n


def split_cols(p, sizes):
    out, o = [], 0
    for s in sizes:
        out.append(p[..., o:o + s])
        o += s
    return out


def rms_norm(x, g):
    xf = x.astype(jnp.float32)
    y = xf * lax.rsqrt(jnp.mean(xf * xf, axis=-1, keepdims=True) + EPS)
    return y.astype(x.dtype) * g


def forgetting_attention(q, cf_q, qpos, k, v, cf_k):
    b, lq = q.shape[0], q.shape[1]
    lk = k.shape[1]
    r = N_HEADS // N_KV_HEADS
    blk = _block(lq, Q_BLOCK)
    nb = lq // blk
    qb = q.reshape(b, nb, blk, N_KV_HEADS, r, HEAD_DIM).transpose(1, 0, 2, 3, 4, 5)
    cqb = cf_q.reshape(b, nb, blk, N_KV_HEADS, r).transpose(1, 0, 3, 4, 2)
    pb = qpos.reshape(nb, blk)
    ck = cf_k.reshape(b, lk, N_KV_HEADS, r).transpose(0, 2, 3, 1)
    kpos = jnp.arange(lk)
    scale = HEAD_DIM ** -0.5

    def one_block(args):
        qi, ci, pi = args
        s = jnp.einsum('bqgrd,bkgd->bgrqk', qi, k).astype(jnp.float32) * scale
        s = s + ci[..., :, None] - ck[..., None, :]
        s = jnp.where(kpos[None, :] <= pi[:, None], s, -jnp.inf)
        p = jax.nn.softmax(s, axis=-1).astype(v.dtype)
        return jnp.einsum('bgrqk,bkgd->bqgrd', p, v)

    o = lax.map(one_block, (qb, cqb, pb))
    return o.transpose(1, 0, 2, 3, 4, 5).reshape(b, lq, ATT_WIDTH)


def ssd_scan(xs, dt, a, bm, cm, s0, chunk):
    b, l, g, r, p = xs.shape
    nc = l // chunk
    ch = lambda t: t.reshape((b, nc, chunk) + t.shape[2:])
    xd = ch(xs * dt[..., None])
    acs = jnp.cumsum(ch(dt * a), axis=2)
    bm, cm = ch(bm), ch(cm)
    tri = jnp.tril(jnp.ones((chunk, chunk), dtype=bool))[:, :, None, None]
    seg = acs[:, :, :, None] - acs[:, :, None, :]
    lmat = jnp.exp(jnp.where(tri, seg, -jnp.inf))
    cb = jnp.einsum('bctgn,bcsgn->bctsg', cm, bm)
    y_diag = jnp.einsum('bctsg,bctsgr,bcsgrp->bctgrp', cb, lmat, xd)
    decay_end = jnp.exp(acs[:, :, -1:] - acs)
    st = jnp.einsum('bcsgn,bcsgr,bcsgrp->cbgrpn', bm, decay_end, xd)
    cdec = jnp.exp(acs[:, :, -1]).transpose(1, 0, 2, 3)

    def step(s, inp):
        st_c, d_c = inp
        return s * d_c[..., None, None] + st_c, s

    s_fin, s_prev = lax.scan(step, s0, (st, cdec))
    y_off = jnp.einsum('bctgn,bctgr,cbgrpn->bctgrp', cm, jnp.exp(acs), s_prev)
    return (y_diag + y_off).reshape(b, l, g, r, p), s_fin


def mamba2_branch(z, xbc, dtr, ssm0, conv0, w):
    b, l = xbc.shape[0], xbc.shape[1]
    r = SSM_HEADS // SSM_GROUPS
    f32 = jnp.float32
    xa = jnp.concatenate([conv0.astype(xbc.dtype), xbc], axis=1)
    conv = w['conv_b'] + sum(xa[:, i:i + l] * w['conv_w'][i] for i in range(CONV_WIDTH))
    conv_new = xa[:, xa.shape[1] - (CONV_WIDTH - 1):]
    u = jax.nn.silu(conv)
    xs, bm, cm = split_cols(u, (D_INNER, SSM_GROUPS * D_STATE, SSM_GROUPS * D_STATE))
    xs = xs.reshape(b, l, SSM_GROUPS, r, SSM_HEAD_DIM).astype(f32)
    bm = bm.reshape(b, l, SSM_GROUPS, D_STATE).astype(f32)
    cm = cm.reshape(b, l, SSM_GROUPS, D_STATE).astype(f32)
    dt = jax.nn.softplus(dtr.astype(f32) + w['dt_bias'].astype(f32)).reshape(b, l, SSM_GROUPS, r)
    a = -jnp.exp(w['a_log'].astype(f32)).reshape(SSM_GROUPS, r)
    s0 = ssm0.astype(f32).reshape(b, SSM_GROUPS, r, SSM_HEAD_DIM, D_STATE)
    y, s_fin = ssd_scan(xs, dt, a, bm, cm, s0, _block(l, SSD_CHUNK))
    y = y + w['d_skip'].astype(f32).reshape(SSM_GROUPS, r)[..., None] * xs
    y = y.reshape(b, l, D_INNER) * jax.nn.silu(z.astype(f32))
    yg = y.reshape(b, l, SSM_GROUPS, D_INNER // SSM_GROUPS)
    yg = yg * lax.rsqrt(jnp.mean(yg * yg, axis=-1, keepdims=True) + EPS)
    y = yg.reshape(b, l, D_INNER).astype(z.dtype) * w['g_ssm']
    s_new = s_fin.reshape(b, SSM_HEADS, SSM_HEAD_DIM, D_STATE).astype(z.dtype)
    return y, s_new, conv_new


def grouped_moe(h, w_router, b_router, w_gate, w_up, w_down):
    shp = h.shape
    t = h.reshape(-1, shp[-1])
    f32 = jnp.float32
    logits = (t @ w_router).astype(f32) + b_router.astype(f32)
    probs = jax.nn.softmax(logits, axis=-1)
    pg = probs.reshape(-1, N_EXPERT_GROUPS, EXPERTS_PER_GROUP)
    gsel = jnp.argmax(jnp.max(pg, axis=-1), axis=-1)
    pin = jnp.take_along_axis(pg, gsel[:, None, None], axis=1)[:, 0]
    vals, idx = lax.top_k(pin, TOP_K)
    wts = vals / jnp.sum(vals, axis=-1, keepdims=True)
    eid = gsel[:, None] * EXPERTS_PER_GROUP + idx
    comb = jnp.sum(jax.nn.one_hot(eid, N_EXPERTS, dtype=f32) * wts[..., None], axis=1)
    act = jax.nn.silu(jnp.einsum('td,edf->tef', t, w_gate)) * jnp.einsum('td,edf->tef', t, w_up)
    act = act * comb.astype(act.dtype)[..., None]
    y = jnp.einsum('tef,efd->td', act, w_down)
    return y.reshape(shp)


def layer(x, c, past, ssm0, conv0, w, w_router, b_router):
    b, l = x.shape[0], x.shape[1]
    mod = jax.nn.silu(c) @ w['w_mod'] + w['b_mod']
    sh1, sc1, gt1, sh2, sc2, gt2 = [m[:, None, :] for m in jnp.split(mod, 6, axis=-1)]
    h = rms_norm(x, w['g_norm1']) * (1 + sc1) + sh1
    proj = h @ w['w_in']
    q, k, v, fl, z, xbc, dtr, ga, gb = split_cols(proj, IN_SIZES)
    q = rms_norm(q.reshape(b, l, N_HEADS, HEAD_DIM), w['g_q'])
    k = rms_norm(k.reshape(b, l, N_KV_HEADS, HEAD_DIM), w['g_k'])
    v = v.reshape(b, l, N_KV_HEADS, HEAD_DIM)
    logf = jax.nn.log_sigmoid(fl.astype(jnp.float32) + w['b_f'].astype(jnp.float32))
    if past is None:
        k_all, v_all, lf_all = k, v, logf
    else:
        k_all = jnp.concatenate([past[0].astype(k.dtype), k], axis=1)
        v_all = jnp.concatenate([past[1].astype(v.dtype), v], axis=1)
        lf_all = jnp.concatenate([past[2].astype(jnp.float32), logf], axis=1)
    cf = jnp.cumsum(lf_all, axis=1)
    qpos = (k_all.shape[1] - l) + jnp.arange(l)
    o_att = forgetting_attention(q, cf[:, cf.shape[1] - l:], qpos, k_all, v_all, cf)
    y_ssm, ssm_new, conv_new = mamba2_branch(z, xbc, dtr, ssm0, conv0, w)
    m = jax.nn.sigmoid(ga) * (o_att @ w['w_pa']) + jax.nn.sigmoid(gb) * (y_ssm @ w['w_pb'])
    x = x + gt1 * (m @ w['w_o'])
    h2 = rms_norm(x, w['g_norm2']) * (1 + sc2) + sh2
    x = x + gt2 * grouped_moe(h2, w_router, b_router, w['w_gate'], w['w_up'], w['w_down'])
    return x, (k, v, logf.astype(x.dtype), ssm_new, conv_new)


def setup_inputs(seed: int = 0) -> dict:
    key = jax.random.key(seed)
    ks = iter(jax.random.split(key, 48))
    f32 = jnp.float32
    n_pages = PAST_LEN // PAGE_SIZE
    n_pool = (DEC_BATCH * n_pages * 5 + 3) // 4

    def nrm(shape, scale):
        return jax.random.normal(next(ks), shape, f32) * scale

    x_prompt = nrm((BATCH, SEQ, D_MODEL), 1.0)
    x_sample = nrm((DEC_BATCH, DEC_SEQ, D_MODEL), 1.0)
    cache_k = nrm((DEPTH, n_pool, PAGE_SIZE, N_KV_HEADS, HEAD_DIM), 1.0)
    cache_v = nrm((DEPTH, n_pool, PAGE_SIZE, N_KV_HEADS, HEAD_DIM), 1.0)
    cache_logf = jax.nn.log_sigmoid(FORGET_BIAS_INIT + nrm((DEPTH, n_pool, PAGE_SIZE, N_HEADS), 1.0))
    state_ssm = nrm((DEPTH, DEC_BATCH, SSM_HEADS, SSM_HEAD_DIM, D_STATE), 0.1)
    state_conv = nrm((DEPTH, DEC_BATCH, CONV_WIDTH - 1, CONV_DIM), 1.0)
    perm = jax.random.permutation(next(ks), n_pool)
    page_table = perm[:DEC_BATCH * n_pages].reshape(DEC_BATCH, n_pages).astype(jnp.int32)
    c_prompt = nrm((BATCH, D_MODEL), 1.0)
    c_sample = nrm((DEC_BATCH, D_MODEL), 1.0)
    w_mod = nrm((DEPTH, D_MODEL, 6 * D_MODEL), 0.5 * D_MODEL ** -0.5)
    b_mod = nrm((DEPTH, 6 * D_MODEL), 0.02)
    g_norm1 = 1.0 + nrm((DEPTH, D_MODEL), 0.05)
    g_norm2 = 1.0 + nrm((DEPTH, D_MODEL), 0.05)
    w_in = nrm((DEPTH, D_MODEL, IN_DIM), D_MODEL ** -0.5)
    b_f = FORGET_BIAS_INIT + nrm((DEPTH, N_HEADS), 0.5)
    g_q = 1.0 + nrm((DEPTH, HEAD_DIM), 0.05)
    g_k = 1.0 + nrm((DEPTH, HEAD_DIM), 0.05)
    conv_w = nrm((DEPTH, CONV_WIDTH, CONV_DIM), CONV_WIDTH ** -0.5)
    conv_b = nrm((DEPTH, CONV_DIM), 0.02)
    dt0 = jnp.exp(jax.random.uniform(next(ks), (DEPTH, SSM_HEADS), f32, math.log(1e-3), math.log(1e-1)))
    dt_bias = dt0 + jnp.log(-jnp.expm1(-dt0))
    a_log = jnp.log(jax.random.uniform(next(ks), (DEPTH, SSM_HEADS), f32, 1.0, 16.0))
    d_skip = 1.0 + nrm((DEPTH, SSM_HEADS), 0.1)
    g_ssm = 1.0 + nrm((DEPTH, D_INNER), 0.05)
    w_pa = nrm((DEPTH, ATT_WIDTH, D_MODEL), ATT_WIDTH ** -0.5)
    w_pb = nrm((DEPTH, D_INNER, D_MODEL), D_INNER ** -0.5)
    w_o = nrm((DEPTH, D_MODEL, D_MODEL), D_MODEL ** -0.5)
    w_router = nrm((D_MODEL, N_EXPERTS), D_MODEL ** -0.5)
    b_router = nrm((N_EXPERTS,), 0.01)
    w_gate = nrm((DEPTH, N_EXPERTS, D_MODEL, D_FF_EXPERT), D_MODEL ** -0.5)
    w_up = nrm((DEPTH, N_EXPERTS, D_MODEL, D_FF_EXPERT), D_MODEL ** -0.5)
    w_down = nrm((DEPTH, N_EXPERTS, D_FF_EXPERT, D_MODEL), D_FF_EXPERT ** -0.5)
    return {'x_prompt': x_prompt, 'x_sample': x_sample, 'cache_k': cache_k, 'cache_v': cache_v,
            'cache_logf': cache_logf, 'state_ssm': state_ssm, 'state_conv': state_conv,
            'page_table': page_table, 'c_prompt': c_prompt, 'c_sample': c_sample,
            'w_mod': w_mod, 'b_mod': b_mod, 'g_norm1': g_norm1, 'g_norm2': g_norm2, 'w_in': w_in,
            'b_f': b_f, 'g_q': g_q, 'g_k': g_k, 'conv_w': conv_w, 'conv_b': conv_b,
            'dt_bias': dt_bias, 'a_log': a_log, 'd_skip': d_skip, 'g_ssm': g_ssm,
            'w_pa': w_pa, 'w_pb': w_pb, 'w_o': w_o, 'w_router': w_router, 'b_router': b_router,
            'w_gate': w_gate, 'w_up': w_up, 'w_down': w_down}


def reference(x_prompt, x_sample, cache_k, cache_v, cache_logf, state_ssm, state_conv, page_table,
              c_prompt, c_sample, w_mod, b_mod, g_norm1, g_norm2, w_in, b_f, g_q, g_k, conv_w, conv_b,
              dt_bias, a_log, d_skip, g_ssm, w_pa, w_pb, w_o, w_router, b_router, w_gate, w_up, w_down):
    bp = x_prompt.shape[0]
    db = x_sample.shape[0]
    yp, ys = x_prompt, x_sample
    kp, vp, lp, sp, cp = [], [], [], [], []
    kq, vq, lq, sq, cq = [], [], [], [], []
    for l in range(DEPTH):
        w = dict(w_mod=w_mod[l], b_mod=b_mod[l], g_norm1=g_norm1[l], g_norm2=g_norm2[l], w_in=w_in[l],
                 b_f=b_f[l], g_q=g_q[l], g_k=g_k[l], conv_w=conv_w[l], conv_b=conv_b[l],
                 dt_bias=dt_bias[l], a_log=a_log[l], d_skip=d_skip[l], g_ssm=g_ssm[l],
                 w_pa=w_pa[l], w_pb=w_pb[l], w_o=w_o[l], w_gate=w_gate[l], w_up=w_up[l], w_down=w_down[l])
        ssm_zero = jnp.zeros((bp, SSM_HEADS, SSM_HEAD_DIM, D_STATE), x_prompt.dtype)
        conv_zero = jnp.zeros((bp, CONV_WIDTH - 1, CONV_DIM), x_prompt.dtype)
        yp, st = layer(yp, c_prompt, None, ssm_zero, conv_zero, w, w_router, b_router)
        kp.append(st[0]); vp.append(st[1]); lp.append(st[2]); sp.append(st[3]); cp.append(st[4])
        past = (cache_k[l][page_table].reshape(db, -1, N_KV_HEADS, HEAD_DIM),
                cache_v[l][page_table].reshape(db, -1, N_KV_HEADS, HEAD_DIM),
                cache_logf[l][page_table].reshape(db, -1, N_HEADS))
        ys, st = layer(ys, c_sample, past, state_ssm[l], state_conv[l], w, w_router, b_router)
        kq.append(st[0]); vq.append(st[1]); lq.append(st[2]); sq.append(st[3]); cq.append(st[4])
    return (yp, ys,
            jnp.stack(kp), jnp.stack(vp), jnp.stack(lp), jnp.stack(sp), jnp.stack(cp),
            jnp.stack(kq), jnp.stack(vq), jnp.stack(lq), jnp.stack(sq), jnp.stack(cq))
```

```python
import functools

import jax
import jax.numpy as jnp
from jax import lax
from jax.experimental import pallas as pl
from jax.experimental.pallas import tpu as pltpu

F32 = jnp.float32
BF16 = jnp.bfloat16
HI = lax.Precision.HIGHEST
EPS = 1e-6
NEG = -1e30

N_HEADS = 16
N_KV_HEADS = 8
HEAD_DIM = 64
ATT_WIDTH = N_HEADS * HEAD_DIM
KV_WIDTH = N_KV_HEADS * HEAD_DIM
SSM_HEADS = 32
SSM_HEAD_DIM = 64
SSM_GROUPS = 4
D_STATE = 128
CONV_WIDTH = 4
D_INNER = SSM_HEADS * SSM_HEAD_DIM
BC_WIDTH = SSM_GROUPS * D_STATE
CONV_DIM = D_INNER + 2 * BC_WIDTH
SSD_CHUNK = 128
N_EXPERTS = 16
N_EXPERT_GROUPS = 4
EXPERTS_PER_GROUP = N_EXPERTS // N_EXPERT_GROUPS

VMEM_LIMIT_BYTES = 56 * 1024 * 1024
LANES = 128

NT_DIMS = (((1,), (1,)), ((), ()))


def _params(*sem):
    return pltpu.CompilerParams(dimension_semantics=sem, vmem_limit_bytes=VMEM_LIMIT_BYTES)


def _sigmoid(x):
    return 1.0 / (1.0 + jnp.exp(-x))


def _silu(x):
    return x * _sigmoid(x)


def _softplus(x):
    return jnp.maximum(x, 0.0) + jnp.log1p(jnp.exp(-jnp.abs(x)))


def _dot(a, b, prec=None):
    return jnp.dot(a, b, precision=prec, preferred_element_type=F32)


def _dot_nt(a, b, prec=None):
    return lax.dot_general(a, b, NT_DIMS, precision=prec, preferred_element_type=F32)


def _eye(n):
    return (lax.broadcasted_iota(jnp.int32, (n, n), 0) == lax.broadcasted_iota(jnp.int32, (n, n), 1)).astype(F32)


def _head_expander(n_heads, width):
    r = lax.broadcasted_iota(jnp.int32, (n_heads, n_heads * width), 0)
    c = lax.broadcasted_iota(jnp.int32, (n_heads, n_heads * width), 1)
    return (c // width == r).astype(F32)


def _mod_kernel(c_ref, w_ref, b_ref, o_ref):
    o_ref[0] = _dot(_silu(c_ref[...]), w_ref[0], HI) + b_ref[0]


def _modulation(c_all, w_mod, b_mod, tn=512):
    depth, d, n = w_mod.shape
    rows = c_all.shape[0]
    return pl.pallas_call(
        _mod_kernel,
        grid=(depth, n // tn),
        in_specs=[pl.BlockSpec((rows, d), lambda l, j: (0, 0)),
                  pl.BlockSpec((1, d, tn), lambda l, j: (l, 0, j)),
                  pl.BlockSpec((1, 1, tn), lambda l, j: (l, 0, j))],
        out_specs=pl.BlockSpec((1, rows, tn), lambda l, j: (l, 0, j)),
        out_shape=jax.ShapeDtypeStruct((depth, rows, n), F32),
        compiler_params=_params("parallel", "parallel"),
        name="modulation",
    )(c_all, w_mod, b_mod.reshape(depth, 1, n))


def _norm_proj_kernel(x_ref, sc_ref, sh_ref, g_ref, w_ref, ws_ref, *refs, starts, prec):
    n_sec = len(starts) - 1
    out_refs, small_ref, h_ref = refs[:n_sec], refs[n_sec], refs[n_sec + 1]
    n = pl.program_id(2)

    @pl.when(n == 0)
    def _():
        x = x_ref[0]
        y = x * lax.rsqrt(jnp.mean(x * x, axis=-1, keepdims=True) + EPS)
        h = (y * g_ref[...]) * (1.0 + sc_ref[0]) + sh_ref[0]
        h = h.astype(h_ref.dtype)
        h_ref[...] = h
        small_ref[0] = _dot(h, ws_ref[...], prec)

    for i in range(n_sec):
        @pl.when(jnp.logical_and(n >= starts[i], n < starts[i + 1]))
        def _(i=i):
            out_refs[i][0] = _dot(h_ref[...], w_ref[...], prec).astype(out_refs[i].dtype)


def _norm_proj(x, sc, sh, g, w_main, w_small, sections, *, tm, tn, prec):
    b, l, d = x.shape
    mod_rows = sc.shape[1]
    mr = tm if mod_rows == l else 1
    starts = [0]
    for nt, _ in sections:
        starts.append(starts[-1] + nt)
    starts = tuple(starts)
    n_tiles = starts[-1]
    assert w_main.shape == (d, n_tiles * tn) and l % tm == 0
    ws_cols = w_small.shape[1]

    def mod_map(bi, mi, ni):
        return (bi, mi if mod_rows == l else 0, 0)

    def sec_map(i):
        lo, cnt = starts[i], sections[i][0]
        return lambda bi, mi, ni: (bi, mi, jnp.clip(ni - lo, 0, cnt - 1))

    out_specs = [pl.BlockSpec((1, tm, tn), sec_map(i)) for i in range(len(sections))]
    out_specs.append(pl.BlockSpec((1, tm, ws_cols), lambda bi, mi, ni: (bi, mi, 0)))
    out_shape = [jax.ShapeDtypeStruct((b, l, nt * tn), dt) for nt, dt in sections]
    out_shape.append(jax.ShapeDtypeStruct((b, l, ws_cols), F32))
    return pl.pallas_call(
        functools.partial(_norm_proj_kernel, starts=starts, prec=prec),
        grid=(b, l // tm, n_tiles),
        in_specs=[pl.BlockSpec((1, tm, d), lambda bi, mi, ni: (bi, mi, 0)),
                  pl.BlockSpec((1, mr, d), mod_map),
                  pl.BlockSpec((1, mr, d), mod_map),
                  pl.BlockSpec((1, d), lambda bi, mi, ni: (0, 0)),
                  pl.BlockSpec((d, tn), lambda bi, mi, ni: (0, ni)),
                  pl.BlockSpec((d, ws_cols), lambda bi, mi, ni: (0, 0))],
        out_specs=out_specs,
        out_shape=out_shape,
        scratch_shapes=[pltpu.VMEM((tm, d), w_main.dtype)],
        compiler_params=_params("parallel", "parallel", "arbitrary"),
        name="norm_proj",
    )(x, sc, sh, g.reshape(1, d), w_main, w_small)


def _head_norm(x, g):
    return x * lax.rsqrt(jnp.mean(x * x, axis=-1, keepdims=True) + EPS) * g


def _log_forget(small_ref, bf_ref):
    fl = small_ref[0][:, :N_HEADS] + bf_ref[...]
    return -_softplus(-fl)


def _qkv_post_prompt_kernel(qkv_ref, small_ref, gq_ref, gk_ref, bf_ref,
                            qh_ref, kh_ref, vh_ref, ko_ref, vo_ref, lf_ref):
    scale = HEAD_DIM ** -0.5
    for h in range(N_HEADS):
        q = qkv_ref[0, :, h * HEAD_DIM:(h + 1) * HEAD_DIM]
        qh_ref[0, h] = (_head_norm(q, gq_ref[...]) * scale).astype(qh_ref.dtype)
    for g in range(N_KV_HEADS):
        lo = ATT_WIDTH + g * HEAD_DIM
        kn = _head_norm(qkv_ref[0, :, lo:lo + HEAD_DIM], gk_ref[...])
        kh_ref[0, g] = kn.astype(kh_ref.dtype)
        ko_ref[0, :, g * HEAD_DIM:(g + 1) * HEAD_DIM] = kn
        lo = ATT_WIDTH + KV_WIDTH + g * HEAD_DIM
        v = qkv_ref[0, :, lo:lo + HEAD_DIM].astype(vh_ref.dtype)
        for r in range(N_HEADS // N_KV_HEADS):
            vh_ref[0, g, :, r * HEAD_DIM:(r + 1) * HEAD_DIM] = v
    vo_ref[0] = qkv_ref[0, :, ATT_WIDTH + KV_WIDTH:ATT_WIDTH + 2 * KV_WIDTH]
    lf_ref[0] = _log_forget(small_ref, bf_ref)


def _qkv_post_prompt(qkv, small, g_q, g_k, b_f, *, tm):
    b, l, _ = qkv.shape
    tok = lambda w: pl.BlockSpec((1, tm, w), lambda bi, mi: (bi, mi, 0))
    head = lambda n, w=HEAD_DIM: pl.BlockSpec((1, n, tm, w), lambda bi, mi: (bi, 0, mi, 0))
    vec = lambda w: pl.BlockSpec((1, w), lambda bi, mi: (0, 0))
    v_width = (N_HEADS // N_KV_HEADS) * HEAD_DIM
    return pl.pallas_call(
        _qkv_post_prompt_kernel,
        grid=(b, l // tm),
        in_specs=[tok(ATT_WIDTH + 2 * KV_WIDTH), tok(LANES), vec(HEAD_DIM), vec(HEAD_DIM), vec(N_HEADS)],
        out_specs=[head(N_HEADS), head(N_KV_HEADS), head(N_KV_HEADS, v_width), tok(KV_WIDTH), tok(KV_WIDTH),
                   tok(N_HEADS)],
        out_shape=[jax.ShapeDtypeStruct((b, N_HEADS, l, HEAD_DIM), BF16),
                   jax.ShapeDtypeStruct((b, N_KV_HEADS, l, HEAD_DIM), BF16),
                   jax.ShapeDtypeStruct((b, N_KV_HEADS, l, v_width), BF16),
                   jax.ShapeDtypeStruct((b, l, KV_WIDTH), F32),
                   jax.ShapeDtypeStruct((b, l, KV_WIDTH), F32),
                   jax.ShapeDtypeStruct((b, l, N_HEADS), F32)],
        compiler_params=_params("parallel", "parallel"),
        name="qkv_post_prompt",
    )(qkv, small, g_q.reshape(1, -1), g_k.reshape(1, -1), b_f.reshape(1, -1))


def _qkv_post_sample_kernel(qkv_ref, small_ref, gq_ref, gk_ref, bf_ref, qo_ref, ko_ref, vo_ref, lf_ref):
    scale = HEAD_DIM ** -0.5
    for h in range(N_HEADS):
        sl = slice(h * HEAD_DIM, (h + 1) * HEAD_DIM)
        qo_ref[0, :, sl] = _head_norm(qkv_ref[0, :, sl], gq_ref[...]) * scale
    for g in range(N_KV_HEADS):
        lo = ATT_WIDTH + g * HEAD_DIM
        ko_ref[0, :, g * HEAD_DIM:(g + 1) * HEAD_DIM] = _head_norm(qkv_ref[0, :, lo:lo + HEAD_DIM], gk_ref[...])
    vo_ref[0] = qkv_ref[0, :, ATT_WIDTH + KV_WIDTH:ATT_WIDTH + 2 * KV_WIDTH]
    lf_ref[0] = _log_forget(small_ref, bf_ref)


def _qkv_post_sample(qkv, small, g_q, g_k, b_f):
    b, l, _ = qkv.shape
    tok = lambda w: pl.BlockSpec((1, l, w), lambda bi: (bi, 0, 0))
    vec = lambda w: pl.BlockSpec((1, w), lambda bi: (0, 0))
    return pl.pallas_call(
        _qkv_post_sample_kernel,
        grid=(b,),
        in_specs=[tok(ATT_WIDTH + 2 * KV_WIDTH), tok(LANES), vec(HEAD_DIM), vec(HEAD_DIM), vec(N_HEADS)],
        out_specs=[tok(ATT_WIDTH), tok(KV_WIDTH), tok(KV_WIDTH), tok(N_HEADS)],
        out_shape=[jax.ShapeDtypeStruct((b, l, ATT_WIDTH), F32),
                   jax.ShapeDtypeStruct((b, l, KV_WIDTH), F32),
                   jax.ShapeDtypeStruct((b, l, KV_WIDTH), F32),
                   jax.ShapeDtypeStruct((b, l, N_HEADS), F32)],
        compiler_params=_params("parallel"),
        name="qkv_post_sample",
    )(qkv, small, g_q.reshape(1, -1), g_k.reshape(1, -1), b_f.reshape(1, -1))


def _cumsum_kernel(lf_ref, cf_ref, cft_ref, *, chunk):
    l = lf_ref.shape[1]
    r = lax.broadcasted_iota(jnp.int32, (chunk, chunk), 0)
    c = lax.broadcasted_iota(jnp.int32, (chunk, chunk), 1)
    tril = (r >= c).astype(F32)
    eye = _eye(N_HEADS)
    carry = jnp.zeros((1, N_HEADS), F32)
    for i in range(l // chunk):
        sl = slice(i * chunk, (i + 1) * chunk)
        loc = _dot(tril, lf_ref[0, sl, :], HI) + carry
        cf_ref[0, sl, :] = loc
        cft_ref[0, :, sl] = _dot_nt(eye, loc, HI)
        carry = loc[chunk - 1:chunk, :]


def _cumsum_logf(lf, *, chunk=256):
    b, l, h = lf.shape
    return pl.pallas_call(
        functools.partial(_cumsum_kernel, chunk=chunk),
        grid=(b,),
        in_specs=[pl.BlockSpec((1, l, h), lambda bi: (bi, 0, 0))],
        out_specs=[pl.BlockSpec((1, l, h), lambda bi: (bi, 0, 0)), pl.BlockSpec((1, h, l), lambda bi: (bi, 0, 0))],
        out_shape=[jax.ShapeDtypeStruct((b, l, h), F32), jax.ShapeDtypeStruct((b, h, l), F32)],
        compiler_params=_params("parallel"),
        name="cumsum_logf",
    )(lf)


ATTN_ROW_CHUNK = 512


def _attn_prompt_kernel(q_ref, k_ref, v_ref, cf_ref, cft_ref, o_ref, m_ref, l_ref, cq_ref, acc_ref, *, tq):
    g, qi = pl.program_id(1), pl.program_id(2)
    rep = N_HEADS // N_KV_HEADS
    tk = tq
    width = rep * HEAD_DIM
    sub = min(ATTN_ROW_CHUNK, tq)
    lane16 = lax.broadcasted_iota(jnp.int32, (tq, N_HEADS), 1)
    cfb = cf_ref[0]
    for r in range(rep):
        cq = jnp.sum(jnp.where(lane16 == rep * g + r, cfb, 0.0), axis=-1, keepdims=True)
        cq_ref[r] = jnp.broadcast_to(cq, (tq, width))
    head_of_lane = lax.broadcasted_iota(jnp.int32, (sub, width), 1) // HEAD_DIM

    def per_head(vals):
        out = vals[0]
        for r in range(1, rep):
            out = jnp.where(head_of_lane == r, vals[r], out)
        return out

    m_ref[...] = jnp.full(m_ref.shape, NEG, F32)
    l_ref[...] = jnp.zeros(l_ref.shape, F32)
    acc_ref[...] = jnp.zeros(acc_ref.shape, F32)

    def block(ki, diagonal):
        keys = pl.ds(pl.multiple_of(ki * tk, tk), tk)

        def chunk(ci, carry):
            r0 = pl.multiple_of(ci * sub, sub)
            rows = pl.ds(r0, sub)
            alphas, pvs = [], []
            for r in range(rep):
                t = _dot_nt(q_ref[0, r, rows, :], k_ref[0, 0, keys, :]) - cft_ref[0, 0, ki, r:r + 1, :]
                if diagonal:
                    row = r0 + lax.broadcasted_iota(jnp.int32, (sub, tk), 0)
                    col = lax.broadcasted_iota(jnp.int32, (sub, tk), 1)
                    t = jnp.where(col <= row, t, NEG)
                cq = cq_ref[r, rows, :]
                m_prev = m_ref[r, rows, :]
                m_new = jnp.maximum(m_prev, jnp.max(t, axis=-1, keepdims=True) + cq)
                alpha = jnp.exp(m_prev - m_new)
                p = jnp.exp(t + jnp.tile(cq - m_new, (1, tk // width)))
                l_ref[r, rows, :] = alpha * l_ref[r, rows, :] + jnp.sum(p, axis=-1, keepdims=True)
                m_ref[r, rows, :] = m_new
                alphas.append(alpha)
                pvs.append(_dot(p.astype(v_ref.dtype), v_ref[0, 0, keys, :]))
            acc_ref[rows, :] = per_head(alphas) * acc_ref[rows, :] + per_head(pvs)
            return carry

        lax.fori_loop(0, tq // sub, chunk, 0, unroll=True)

    def body(ki, carry):
        block(ki, False)
        return carry

    lax.fori_loop(0, qi, body, 0)
    block(qi, True)

    def finish(ci, carry):
        rows = pl.ds(pl.multiple_of(ci * sub, sub), sub)
        denom = per_head([l_ref[r, rows, :] for r in range(rep)])
        o_ref[0, rows, :] = (acc_ref[rows, :] / denom).astype(o_ref.dtype)
        return carry

    lax.fori_loop(0, tq // sub, finish, 0)


def _attention_prompt(qh, kh, vh_rep, cf, cft, *, tq):
    b, _, l, _ = qh.shape
    rep = N_HEADS // N_KV_HEADS
    width = rep * HEAD_DIM
    nq = l // tq
    cft_tiles = cft.reshape(b, N_KV_HEADS, rep, nq, tq).transpose(0, 1, 3, 2, 4)
    return pl.pallas_call(
        functools.partial(_attn_prompt_kernel, tq=tq),
        grid=(b, N_KV_HEADS, nq),
        in_specs=[pl.BlockSpec((1, rep, tq, HEAD_DIM), lambda bi, g, qi: (bi, g, qi, 0)),
                  pl.BlockSpec((1, 1, l, HEAD_DIM), lambda bi, g, qi: (bi, g, 0, 0)),
                  pl.BlockSpec((1, 1, l, width), lambda bi, g, qi: (bi, g, 0, 0)),
                  pl.BlockSpec((1, tq, N_HEADS), lambda bi, g, qi: (bi, qi, 0)),
                  pl.BlockSpec((1, 1, nq, rep, tq), lambda bi, g, qi: (bi, g, 0, 0, 0))],
        out_specs=pl.BlockSpec((1, tq, width), lambda bi, g, qi: (bi, qi, g)),
        out_shape=jax.ShapeDtypeStruct((b, l, ATT_WIDTH), BF16),
        scratch_shapes=[pltpu.VMEM((rep, tq, width), F32), pltpu.VMEM((rep, tq, width), F32),
                        pltpu.VMEM((rep, tq, width), F32), pltpu.VMEM((tq, width), F32)],
        compiler_params=_params("parallel", "parallel", "arbitrary"),
        name="attention_prompt",
    )(qh, kh, vh_rep, cf, cft_tiles)


def _attn_decode_kernel(pt_ref, qt_ref, q_ref, kn_ref, vnt_ref, lfn_ref, *refs, n_pages, page):
    del pt_ref
    k_refs, v_refs, lf_refs = refs[:n_pages], refs[n_pages:2 * n_pages], refs[2 * n_pages:3 * n_pages]
    ot_ref, qb_ref, s_ref = refs[3 * n_pages:]
    rep = N_HEADS // N_KV_HEADS

    hrow = lax.broadcasted_iota(jnp.int32, (N_HEADS, N_HEADS * page), 0)
    hcol = lax.broadcasted_iota(jnp.int32, (N_HEADS, N_HEADS * page), 1) // page
    q_all = _dot(qt_ref[0], (hrow == hcol).astype(F32), HI)
    for h in range(N_HEADS):
        qb_ref[h] = q_all[:, h * page:(h + 1) * page]

    j = lax.broadcasted_iota(jnp.int32, (page, page), 0)
    t = lax.broadcasted_iota(jnp.int32, (page, page), 1)
    later = (j > t).astype(F32)
    totals = [jnp.sum(lf_refs[p][0, 0], axis=-1, keepdims=True) for p in range(n_pages)]
    carry = lfn_ref[0]
    for p in reversed(range(n_pages)):
        s_ref[:, p * page:(p + 1) * page] = _dot(lf_refs[p][0, 0], later, HI) + carry
        carry = carry + totals[p]

    for p in range(n_pages):
        cols = slice(p * page, (p + 1) * page)
        for g in range(N_KV_HEADS):
            kt = k_refs[p][0, 0, g]
            for r in range(rep):
                h = g * rep + r
                s_ref[h:h + 1, cols] = s_ref[h:h + 1, cols] + jnp.sum(kt * qb_ref[h], axis=0, keepdims=True)

    s = s_ref[...]
    s_new = jnp.sum(q_ref[0] * kn_ref[0], axis=-1, keepdims=True)
    m = jnp.maximum(jnp.max(s, axis=-1, keepdims=True), s_new)
    pmat = jnp.exp(s - m)
    e_new = jnp.exp(s_new - m)
    denom = jnp.sum(pmat, axis=-1, keepdims=True) + e_new
    s_ref[...] = pmat

    lane = lax.broadcasted_iota(jnp.int32, (HEAD_DIM, N_HEADS), 1)
    ot = jnp.zeros((HEAD_DIM, N_HEADS), F32)
    for g in range(N_KV_HEADS):
        acc = [jnp.zeros((HEAD_DIM, page), F32) for _ in range(rep)]
        for p in range(n_pages):
            vt = v_refs[p][0, 0, g]
            for r in range(rep):
                h = g * rep + r
                acc[r] = acc[r] + vt * s_ref[h:h + 1, p * page:(p + 1) * page]
        for r in range(rep):
            ot = jnp.where(lane == g * rep + r, jnp.sum(acc[r], axis=-1, keepdims=True), ot)
    eye = _eye(N_HEADS)
    e_row = jnp.sum(eye * e_new, axis=0, keepdims=True)
    d_row = jnp.sum(eye * denom, axis=0, keepdims=True)
    ot_ref[0] = (ot + vnt_ref[0] * e_row) / d_row


def _attention_decode(layer, page_table, q, k_new, v_new, lf_new, cache_kt, cache_vt, cache_lft):
    db, n_pages = page_table.shape
    page = cache_kt.shape[-1]
    rep = N_HEADS // N_KV_HEADS
    qt = jnp.swapaxes(q, 1, 2)
    kn = jnp.repeat(k_new, rep, axis=1)
    vnt = jnp.swapaxes(jnp.repeat(v_new, rep, axis=1), 1, 2)
    rows = lambda: pl.BlockSpec((1, N_HEADS, HEAD_DIM), lambda bi, pt: (bi, 0, 0))
    cols = lambda: pl.BlockSpec((1, HEAD_DIM, N_HEADS), lambda bi, pt: (bi, 0, 0))
    kv_spec = lambda p: pl.BlockSpec((1, 1, N_KV_HEADS, HEAD_DIM, page),
                                     lambda bi, pt: (layer, pt[bi, p], 0, 0, 0))
    lf_spec = lambda p: pl.BlockSpec((1, 1, N_HEADS, page), lambda bi, pt: (layer, pt[bi, p], 0, 0))
    pages = range(n_pages)
    grid_spec = pltpu.PrefetchScalarGridSpec(
        num_scalar_prefetch=1,
        grid=(db,),
        in_specs=[cols(), rows(), rows(), cols(), pl.BlockSpec((1, N_HEADS, 1), lambda bi, pt: (bi, 0, 0))]
        + [kv_spec(p) for p in pages] + [kv_spec(p) for p in pages] + [lf_spec(p) for p in pages],
        out_specs=cols(),
        scratch_shapes=[pltpu.VMEM((N_HEADS, HEAD_DIM, page), F32), pltpu.VMEM((N_HEADS, n_pages * page), F32)])
    return pl.pallas_call(
        functools.partial(_attn_decode_kernel, n_pages=n_pages, page=page),
        grid_spec=grid_spec,
        out_shape=jax.ShapeDtypeStruct((db, HEAD_DIM, N_HEADS), F32),
        compiler_params=_params("parallel"),
        name="attention_decode",
    )(page_table, qt, q, kn, vnt, lf_new[:, :, None],
      *([cache_kt] * n_pages), *([cache_vt] * n_pages), *([cache_lft] * n_pages))


def _ssd_prompt_kernel(xbc_ref, z_ref, dt_ref, cw_ref, cb_ref, dtb_ref, alog_ref, dskip_ref, gssm_ref,
                       y_ref, st_ref, cv_ref, ext_ref, state_ref, yd_ref):
    c = pl.program_id(1)
    t = SSD_CHUNK
    hist = 8
    hpg = SSM_HEADS // SSM_GROUPS
    gw = hpg * SSM_HEAD_DIM

    @pl.when(c == 0)
    def _():
        ext_ref[0:hist, :] = jnp.zeros((hist, CONV_DIM), F32)
        state_ref[...] = jnp.zeros(state_ref.shape, F32)

    ext_ref[hist:hist + t, :] = xbc_ref[0].astype(F32)
    conv = cb_ref[...] + ext_ref[hist:hist + t, :] * cw_ref[CONV_WIDTH - 1:CONV_WIDTH, :]
    for j in range(1, CONV_WIDTH):
        conv = conv + ext_ref[hist - j:hist - j + t, :] * cw_ref[CONV_WIDTH - 1 - j:CONV_WIDTH - j, :]
    tail = ext_ref[t:t + hist, :]
    cv_ref[0] = tail[hist - (CONV_WIDTH - 1):, :]
    ext_ref[0:hist, :] = tail
    u = _silu(conv)
    xs = u[:, :D_INNER]

    dt = _softplus(dt_ref[0][:, :SSM_HEADS] + dtb_ref[...])
    a = -jnp.exp(alog_ref[...])
    r = lax.broadcasted_iota(jnp.int32, (t, t), 0)
    cidx = lax.broadcasted_iota(jnp.int32, (t, t), 1)
    tri = r >= cidx
    acs = _dot(tri.astype(F32), dt * a, HI)
    acs_t = _dot_nt(_eye(SSM_HEADS), acs, HI)
    expand = _head_expander(SSM_HEADS, SSM_HEAD_DIM).astype(BF16)

    def per_channel(x):
        hi = x.astype(BF16)
        return _dot(hi, expand) + _dot((x - hi.astype(F32)).astype(BF16), expand)

    dt_x = per_channel(dt)
    ea_x = per_channel(jnp.exp(acs))
    dec_x = per_channel(jnp.exp(acs[t - 1:t, :] - acs))
    xd = xs * dt_x
    xdd = (xd * dec_x).astype(BF16)
    xd_b = xd.astype(BF16)
    cdec_x = ea_x[t - 1:t, :]

    for g in range(SSM_GROUPS):
        bg = u[:, D_INNER + g * D_STATE:D_INNER + (g + 1) * D_STATE]
        cg = u[:, D_INNER + BC_WIDTH + g * D_STATE:D_INNER + BC_WIDTH + (g + 1) * D_STATE]
        bg_b, cg_b = bg.astype(BF16), cg.astype(BF16)
        cb = _dot_nt(cg_b, bg_b)
        gs = slice(g * gw, (g + 1) * gw)
        s_prev = state_ref[g]
        y_off = _dot(cg_b, s_prev.astype(BF16)) * ea_x[:, gs]
        state_ref[g] = s_prev * cdec_x[:, gs] + _dot(bg.T.astype(BF16), xdd[:, gs])
        for hh in range(hpg):
            h = g * hpg + hh
            seg = acs[:, h:h + 1] - acs_t[h:h + 1, :]
            lmat = jnp.exp(jnp.where(tri, seg, NEG))
            hs = slice(h * SSM_HEAD_DIM, (h + 1) * SSM_HEAD_DIM)
            yd_ref[:, hs] = _dot((cb * lmat).astype(BF16), xd_b[:, hs])
        yd_ref[:, gs] = yd_ref[:, gs] + y_off

    y = (yd_ref[...] + dskip_ref[...] * xs) * _silu(z_ref[0].astype(F32))
    for g in range(SSM_GROUPS):
        gs = slice(g * gw, (g + 1) * gw)
        yg = y[:, gs]
        yg = yg * lax.rsqrt(jnp.mean(yg * yg, axis=-1, keepdims=True) + EPS)
        y_ref[0, :, gs] = (yg * gssm_ref[:, gs]).astype(y_ref.dtype)

    @pl.when(c == pl.num_programs(1) - 1)
    def _():
        for g in range(SSM_GROUPS):
            st_ref[0, g * gw:(g + 1) * gw, :] = state_ref[g].T


def _ssd_prompt(xbc, z, dtr, conv_w, conv_b, dt_bias, a_log, d_skip, g_ssm):
    b, l, _ = xbc.shape
    t = SSD_CHUNK
    gw = D_INNER // SSM_GROUPS
    vec = lambda w: pl.BlockSpec((1, w), lambda bi, ci: (0, 0))
    y, st, cv = pl.pallas_call(
        _ssd_prompt_kernel,
        grid=(b, l // t),
        in_specs=[pl.BlockSpec((1, t, CONV_DIM), lambda bi, ci: (bi, ci, 0)),
                  pl.BlockSpec((1, t, D_INNER), lambda bi, ci: (bi, ci, 0)),
                  pl.BlockSpec((1, t, LANES), lambda bi, ci: (bi, ci, 1)),
                  pl.BlockSpec((CONV_WIDTH, CONV_DIM), lambda bi, ci: (0, 0)),
                  vec(CONV_DIM), vec(SSM_HEADS), vec(SSM_HEADS), vec(D_INNER), vec(D_INNER)],
        out_specs=[pl.BlockSpec((1, t, D_INNER), lambda bi, ci: (bi, ci, 0)),
                   pl.BlockSpec((1, D_INNER, D_STATE), lambda bi, ci: (bi, 0, 0)),
                   pl.BlockSpec((1, CONV_WIDTH - 1, CONV_DIM), lambda bi, ci: (bi, 0, 0))],
        out_shape=[jax.ShapeDtypeStruct((b, l, D_INNER), BF16),
                   jax.ShapeDtypeStruct((b, D_INNER, D_STATE), F32),
                   jax.ShapeDtypeStruct((b, CONV_WIDTH - 1, CONV_DIM), F32)],
        scratch_shapes=[pltpu.VMEM((t + 8, CONV_DIM), F32),
                        pltpu.VMEM((SSM_GROUPS, D_STATE, gw), F32),
                        pltpu.VMEM((t, D_INNER), F32)],
        compiler_params=_params("parallel", "arbitrary"),
        name="ssd_prompt",
    )(xbc, z, dtr, conv_w, conv_b.reshape(1, -1), dt_bias.reshape(1, -1), a_log.reshape(1, -1),
      jnp.repeat(d_skip, SSM_HEAD_DIM).reshape(1, -1), g_ssm.reshape(1, -1))
    return y, st.reshape(b, SSM_HEADS, SSM_HEAD_DIM, D_STATE), cv


def _ssm_step_prep_kernel(xbc_ref, conv0_ref, dt_ref, cw_ref, cb_ref, dtb_ref, alog_ref, dskip_ref,
                          convn_ref, xd3_ref, da_ref, b_ref, c_ref, dx_ref):
    w = CONV_WIDTH
    db = xbc_ref.shape[0]
    xbc = xbc_ref[...]
    conv = cb_ref[...] + xbc * cw_ref[w - 1:w, :]
    for i in range(w - 1):
        conv = conv + conv0_ref[0, i] * cw_ref[i:i + 1, :]
    for i in range(w - 2):
        convn_ref[i] = conv0_ref[0, i + 1]
    convn_ref[w - 2] = xbc
    u = _silu(conv)
    xs = u[:, :D_INNER]
    b_ref[...] = u[:, D_INNER:D_INNER + BC_WIDTH]
    c_ref[...] = u[:, D_INNER + BC_WIDTH:]
    dt = _softplus(dt_ref[...][:, :SSM_HEADS] + dtb_ref[...])
    da_ref[...] = jnp.exp(dt * -jnp.exp(alog_ref[...]))
    xd = xs * _dot(dt, _head_expander(SSM_HEADS, SSM_HEAD_DIM), HI)
    dx_ref[...] = dskip_ref[...] * xs
    for j in range(D_INNER // LANES):
        sl = slice(j * LANES, (j + 1) * LANES)
        xt = xd[:, sl].T
        hi = xt.astype(BF16)
        rest = xt - hi.astype(F32)
        mid = rest.astype(BF16)
        xd3_ref[sl, 0:db] = hi
        xd3_ref[sl, db:2 * db] = mid
        xd3_ref[sl, 2 * db:3 * db] = (rest - mid.astype(F32)).astype(BF16)


def _ssm_step_prep(layer, xbc, conv_state, dtr, conv_w, conv_b, dt_bias, a_log, d_skip):
    db = xbc.shape[0]
    assert db == LANES
    full = lambda *s: pl.BlockSpec(s, lambda i: tuple(0 for _ in s))
    hist = CONV_WIDTH - 1
    return pl.pallas_call(
        _ssm_step_prep_kernel,
        grid=(1,),
        in_specs=[full(db, CONV_DIM), pl.BlockSpec((1, hist, db, CONV_DIM), lambda i: (layer, 0, 0, 0)),
                  pl.BlockSpec((db, LANES), lambda i: (0, 1)),
                  full(CONV_WIDTH, CONV_DIM), full(1, CONV_DIM), full(1, SSM_HEADS), full(1, SSM_HEADS),
                  full(1, D_INNER)],
        out_specs=[full(hist, db, CONV_DIM), full(D_INNER, 3 * db), full(db, SSM_HEADS), full(db, BC_WIDTH),
                   full(db, BC_WIDTH), full(db, D_INNER)],
        out_shape=[jax.ShapeDtypeStruct((hist, db, CONV_DIM), F32),
                   jax.ShapeDtypeStruct((D_INNER, 3 * db), BF16),
                   jax.ShapeDtypeStruct((db, SSM_HEADS), F32),
                   jax.ShapeDtypeStruct((db, BC_WIDTH), F32),
                   jax.ShapeDtypeStruct((db, BC_WIDTH), F32),
                   jax.ShapeDtypeStruct((db, D_INNER), F32)],
        compiler_params=_params("arbitrary"),
        name="ssm_step_prep",
    )(xbc, conv_state, dtr, conv_w, conv_b.reshape(1, -1), dt_bias.reshape(1, -1), a_log.reshape(1, -1),
      jnp.repeat(d_skip, SSM_HEAD_DIM).reshape(1, -1))


def _ssm_step_state_kernel(da_ref, s_ref, xd3_ref, b_ref, c_ref, *refs):
    so_ref, yt_ref = refs[-2:]
    s_ref, so_ref = s_ref.at[0], so_ref.at[0]
    bi = pl.program_id(0)
    db = xd3_ref.shape[1] // 3
    hpg = SSM_HEADS // SSM_GROUPS

    @pl.when(bi == 0)
    def _():
        yt_ref[...] = jnp.zeros(yt_ref.shape, F32)

    pick = lax.broadcasted_iota(jnp.int32, (3 * db, D_STATE), 0) % db == bi
    xcol = _dot(xd3_ref[...], jnp.where(pick, 1.0, 0.0).astype(BF16))
    lane = lax.broadcasted_iota(jnp.int32, (SSM_HEAD_DIM, db), 1)
    for h in range(SSM_HEADS):
        rows = slice(h * SSM_HEAD_DIM, (h + 1) * SSM_HEAD_DIM)
        ns = slice((h // hpg) * D_STATE, (h // hpg + 1) * D_STATE)
        s_new = s_ref[0, rows, :] * da_ref[bi, h] + xcol[rows, :] * b_ref[0, :, ns]
        so_ref[0, rows, :] = s_new
        y_col = jnp.sum(s_new * c_ref[0, :, ns], axis=-1, keepdims=True)
        yt_ref[rows, :] = jnp.where(lane == bi, y_col, yt_ref[rows, :])


def _ssm_step_state(layer, state_all, new_state_all, xd3, da, bm, cm):
    depth, db, rows, _ = state_all.shape
    seq3 = lambda w: pl.BlockSpec((1, 1, w), lambda bi: (bi, 0, 0))
    st_spec = pl.BlockSpec((1, 1, rows, D_STATE), lambda bi: (layer, bi, 0, 0))
    in_specs = [pl.BlockSpec(memory_space=pltpu.SMEM), st_spec, pl.BlockSpec((rows, 3 * db), lambda bi: (0, 0)),
                seq3(BC_WIDTH), seq3(BC_WIDTH)]
    args = [da, state_all, xd3, bm[:, None, :], cm[:, None, :]]
    aliases = {}
    if new_state_all is not None:
        in_specs.append(pl.BlockSpec(memory_space=pl.ANY))
        args.append(new_state_all)
        aliases = {len(args) - 1: 0}
    return pl.pallas_call(
        _ssm_step_state_kernel,
        grid=(db,),
        in_specs=in_specs,
        out_specs=[st_spec, pl.BlockSpec((rows, db), lambda bi: (0, 0))],
        out_shape=[jax.ShapeDtypeStruct(state_all.shape, F32), jax.ShapeDtypeStruct((rows, db), F32)],
        input_output_aliases=aliases,
        compiler_params=_params("arbitrary"),
        name="ssm_step_state",
    )(*args)


def _ssm_step_gate_kernel(yt_ref, dx_ref, z_ref, gssm_ref, o_ref):
    gw = D_INNER // SSM_GROUPS
    for g in range(SSM_GROUPS):
        gs = slice(g * gw, (g + 1) * gw)
        y = jnp.concatenate([yt_ref[g * gw + j * LANES:g * gw + (j + 1) * LANES, :].T for j in range(gw // LANES)],
                            axis=-1)
        yg = (y + dx_ref[:, gs]) * _silu(z_ref[:, gs])
        o_ref[:, gs] = yg * lax.rsqrt(jnp.mean(yg * yg, axis=-1, keepdims=True) + EPS) * gssm_ref[:, gs]


def _ssm_step_gate(yt, dx, z, g_ssm):
    db = dx.shape[0]
    assert db == LANES
    full = lambda *s: pl.BlockSpec(s, lambda i: tuple(0 for _ in s))
    return pl.pallas_call(
        _ssm_step_gate_kernel,
        grid=(1,),
        in_specs=[full(D_INNER, db), full(db, D_INNER), full(db, D_INNER), full(1, D_INNER)],
        out_specs=full(db, D_INNER),
        out_shape=jax.ShapeDtypeStruct((db, D_INNER), F32),
        compiler_params=_params("arbitrary"),
        name="ssm_step_gate",
    )(yt, dx, z, g_ssm.reshape(1, -1))


def _route(logits):
    e = jnp.exp(logits - jnp.max(logits, axis=-1, keepdims=True))
    probs = e / jnp.sum(e, axis=-1, keepdims=True)
    lane_i = lax.broadcasted_iota(jnp.int32, probs.shape, 1)
    lane = lane_i.astype(F32)
    grp = (lane_i // EXPERTS_PER_GROUP).astype(F32)
    best = jnp.max(jnp.where(grp == 0.0, probs, -1.0), axis=-1, keepdims=True)
    sel = jnp.zeros(best.shape, F32)
    for g in range(1, N_EXPERT_GROUPS):
        gmax = jnp.max(jnp.where(grp == float(g), probs, -1.0), axis=-1, keepdims=True)
        sel = jnp.where(gmax > best, float(g), sel)
        best = jnp.maximum(best, gmax)
    pin = jnp.where(grp == sel, probs, -1.0)
    v1 = jnp.max(pin, axis=-1, keepdims=True)
    i1 = jnp.min(jnp.where(pin == v1, lane, float(N_EXPERTS)), axis=-1, keepdims=True)
    pin2 = jnp.where(lane == i1, -1.0, pin)
    v2 = jnp.max(pin2, axis=-1, keepdims=True)
    i2 = jnp.min(jnp.where(pin2 == v2, lane, float(N_EXPERTS)), axis=-1, keepdims=True)
    tot = v1 + v2
    return jnp.where(lane == i1, v1 / tot, 0.0) + jnp.where(lane == i2, v2 / tot, 0.0)


def _merge_kernel(oa_ref, ys_ref, ga_ref, gb_ref, x_ref, gt_ref, sc_ref, sh_ref, g2_ref,
                  wpa_ref, wpb_ref, wo_ref, wr_ref, br_ref, x1_ref, h2_ref, comb_ref, *, prec):
    cdt = wpa_ref.dtype
    pa = _dot(oa_ref[0].astype(cdt), wpa_ref[0], prec)
    pb = _dot(ys_ref[0].astype(cdt), wpb_ref[0], prec)
    m = _sigmoid(ga_ref[0].astype(F32)) * pa + _sigmoid(gb_ref[0].astype(F32)) * pb
    x1 = x_ref[0] + gt_ref[0] * _dot(m.astype(cdt), wo_ref[0], prec)
    x1_ref[0] = x1
    y = x1 * lax.rsqrt(jnp.mean(x1 * x1, axis=-1, keepdims=True) + EPS)
    h2 = (y * g2_ref[...]) * (1.0 + sc_ref[0]) + sh_ref[0]
    h2_ref[0] = h2.astype(h2_ref.dtype)
    comb_ref[0] = _route(_dot(h2, wr_ref[...], HI) + br_ref[...])


def _merge(layer, o_att, y_ssm, gates, x, gt1, sc2, sh2, g2, w_pa, w_pb, w_o, w_router, b_router, *,
           tm, h2_dtype, prec):
    b, l, d = x.shape
    mod_rows = gt1.shape[1]
    mr = tm if mod_rows == l else 1
    tok = lambda w, j=0: pl.BlockSpec((1, tm, w), lambda bi, mi: (bi, mi, j))
    mod = pl.BlockSpec((1, mr, d), lambda bi, mi: (bi, mi if mod_rows == l else 0, 0))
    full = lambda a: pl.BlockSpec(a.shape, lambda bi, mi: tuple(0 for _ in a.shape))
    per_layer = lambda a: pl.BlockSpec((1,) + a.shape[1:], lambda bi, mi: (layer, 0, 0))
    g2r, brr = g2.reshape(1, d), b_router.reshape(1, -1)
    return pl.pallas_call(
        functools.partial(_merge_kernel, prec=prec),
        grid=(b, l // tm),
        in_specs=[tok(ATT_WIDTH), tok(D_INNER), tok(d, 0), tok(d, 1), tok(d), mod, mod, mod, full(g2r),
                  per_layer(w_pa), per_layer(w_pb), per_layer(w_o), full(w_router), full(brr)],
        out_specs=[tok(d), tok(d), tok(N_EXPERTS)],
        out_shape=[jax.ShapeDtypeStruct((b, l, d), F32),
                   jax.ShapeDtypeStruct((b, l, d), h2_dtype),
                   jax.ShapeDtypeStruct((b, l, N_EXPERTS), F32)],
        compiler_params=_params("parallel", "parallel"),
        name="merge",
    )(o_att, y_ssm, gates, gates, x, gt1, sc2, sh2, g2r, w_pa, w_pb, w_o, w_router, brr)


def _moe_kernel(h_ref, comb_ref, x_ref, gt_ref, wg_ref, wu_ref, wd_ref, o_ref, acc_ref, *, prec):
    e = pl.program_id(2)

    @pl.when(e == 0)
    def _():
        acc_ref[...] = jnp.zeros(acc_ref.shape, F32)

    h = h_ref[0]
    comb = comb_ref[0]
    lane = lax.broadcasted_iota(jnp.int32, comb.shape, 1)
    w_e = jnp.sum(jnp.where(lane == e, comb, 0.0), axis=-1, keepdims=True)
    act = _silu(_dot(h, wg_ref[0, 0], prec)) * _dot(h, wu_ref[0, 0], prec) * w_e
    acc_ref[...] += _dot(act.astype(wd_ref.dtype), wd_ref[0, 0], prec)

    @pl.when(e == pl.num_programs(2) - 1)
    def _():
        o_ref[0] = x_ref[0] + gt_ref[0] * acc_ref[...]


def _moe(layer, h2, comb, x1, gt2, w_gate, w_up, w_down, *, tm, prec):
    b, l, d = x1.shape
    _, ne, _, dff = w_gate.shape
    mod_rows = gt2.shape[1]
    mr = tm if mod_rows == l else 1
    tok = lambda w: pl.BlockSpec((1, tm, w), lambda bi, mi, e: (bi, mi, 0))
    return pl.pallas_call(
        functools.partial(_moe_kernel, prec=prec),
        grid=(b, l // tm, ne),
        in_specs=[tok(d), tok(N_EXPERTS), tok(d),
                  pl.BlockSpec((1, mr, d), lambda bi, mi, e: (bi, mi if mod_rows == l else 0, 0)),
                  pl.BlockSpec((1, 1, d, dff), lambda bi, mi, e: (layer, e, 0, 0)),
                  pl.BlockSpec((1, 1, d, dff), lambda bi, mi, e: (layer, e, 0, 0)),
                  pl.BlockSpec((1, 1, dff, d), lambda bi, mi, e: (layer, e, 0, 0))],
        out_specs=tok(d),
        out_shape=jax.ShapeDtypeStruct((b, l, d), F32),
        scratch_shapes=[pltpu.VMEM((tm, d), F32)],
        compiler_params=_params("parallel", "parallel", "arbitrary"),
        name="moe",
    )(h2, comb, x1, gt2, w_gate, w_up, w_down)


TM_PROJ = 1024
TM_POST = 256
TQ_ATTN = 512
TM_MERGE = 512
TM_MOE = 1024
IN_TILE = 512


def _split_w_in(w):
    o_q, o_k, o_v = 0, ATT_WIDTH, ATT_WIDTH + KV_WIDTH
    o_f = o_v + KV_WIDTH
    o_z = o_f + N_HEADS
    o_x = o_z + D_INNER
    o_dt = o_x + CONV_DIM
    o_ga = o_dt + SSM_HEADS
    d_model = w.shape[0]
    main = jnp.concatenate([w[:, o_q:o_f], w[:, o_z:o_x], w[:, o_x:o_dt], w[:, o_ga:]], axis=1)
    small = jnp.zeros((d_model, 2 * LANES), w.dtype)
    small = small.at[:, :N_HEADS].set(w[:, o_f:o_z]).at[:, LANES:LANES + SSM_HEADS].set(w[:, o_dt:o_ga])
    return main, small


def _sections(rest_dtype):
    qkv = (ATT_WIDTH + 2 * KV_WIDTH) // IN_TILE
    return ((qkv, F32), (D_INNER // IN_TILE, rest_dtype), (CONV_DIM // IN_TILE, rest_dtype),
            (2 * ATT_WIDTH // IN_TILE, rest_dtype))


def kernel(x_prompt, x_sample, cache_k, cache_v, cache_logf, state_ssm, state_conv, page_table, c_prompt, c_sample,
           w_mod, b_mod, g_norm1, g_norm2, w_in, b_f, g_q, g_k, conv_w, conv_b, dt_bias, a_log, d_skip, g_ssm,
           w_pa, w_pb, w_o, w_router, b_router, w_gate, w_up, w_down):
    depth = w_in.shape[0]
    bp, seq, d = x_prompt.shape
    db = x_sample.shape[0]
    assert x_sample.shape[1] == 1

    mod = _modulation(jnp.concatenate([c_prompt, c_sample], axis=0), w_mod, b_mod)
    yp = x_prompt
    ys = x_sample.reshape(1, db, d)
    outs = [[] for _ in range(9)]
    st_s = None

    cache_kt = jnp.transpose(cache_k, (0, 1, 3, 4, 2))
    cache_vt = jnp.transpose(cache_v, (0, 1, 3, 4, 2))
    cache_lft = jnp.transpose(cache_logf, (0, 1, 3, 2))
    conv_state = jnp.transpose(state_conv, (0, 2, 1, 3))
    ssm_state = state_ssm.reshape(depth, db, D_INNER, D_STATE)
    bf = lambda w: w.astype(BF16)
    w_pa_b, w_pb_b, w_o_b, w_gate_b, w_up_b, w_down_b = bf(w_pa), bf(w_pb), bf(w_o), bf(w_gate), bf(w_up), bf(w_down)

    for l in range(depth):
        mp = [mod[l, :bp, i * d:(i + 1) * d][:, None, :] for i in range(6)]
        ms = [mod[l, bp:, i * d:(i + 1) * d][None] for i in range(6)]
        w_main, w_small = _split_w_in(w_in[l])

        qkv, z, xbc, gates, small = _norm_proj(
            yp, mp[1], mp[0], g_norm1[l], w_main.astype(BF16), w_small.astype(BF16), _sections(BF16),
            tm=min(TM_PROJ, seq), tn=IN_TILE, prec=None)
        qh, kh, vh, k_out, v_out, lf = _qkv_post_prompt(qkv, small, g_q[l], g_k[l], b_f[l], tm=min(TM_POST, seq))
        cf, cft = _cumsum_logf(lf)
        o_att = _attention_prompt(qh, kh, vh, cf, cft, tq=min(TQ_ATTN, seq))
        y_ssm, st_p, cv_p = _ssd_prompt(xbc, z, small, conv_w[l], conv_b[l], dt_bias[l], a_log[l], d_skip[l],
                                        g_ssm[l])
        x1, h2, comb = _merge(l, o_att, y_ssm, gates, yp, mp[2], mp[4], mp[3], g_norm2[l],
                              w_pa_b, w_pb_b, w_o_b, w_router, b_router,
                              tm=min(TM_MERGE, seq), h2_dtype=BF16, prec=None)
        yp = _moe(l, h2, comb, x1, mp[5], w_gate_b, w_up_b, w_down_b, tm=min(TM_MOE, seq), prec=None)
        outs[0].append(k_out.reshape(bp, seq, N_KV_HEADS, HEAD_DIM))
        outs[1].append(v_out.reshape(bp, seq, N_KV_HEADS, HEAD_DIM))
        outs[2].append(lf)
        outs[3].append(st_p)
        outs[4].append(cv_p)

        qkv, z, xbc, gates, small = _norm_proj(
            ys, ms[1], ms[0], g_norm1[l], w_main, w_small, _sections(F32), tm=db, tn=IN_TILE, prec=HI)
        qn, k_new, v_new, lf_new = _qkv_post_sample(qkv, small, g_q[l], g_k[l], b_f[l])
        o_att_t = _attention_decode(l, page_table, qn.reshape(db, N_HEADS, HEAD_DIM),
                                    k_new.reshape(db, N_KV_HEADS, HEAD_DIM), v_new.reshape(db, N_KV_HEADS, HEAD_DIM),
                                    lf_new[0], cache_kt, cache_vt, cache_lft)
        o_att = jnp.swapaxes(o_att_t, 1, 2).reshape(1, db, ATT_WIDTH)
        conv_new, xd3, da, bm, cm, dx = _ssm_step_prep(
            l, xbc[0], conv_state, small[0], conv_w[l], conv_b[l], dt_bias[l], a_log[l], d_skip[l])
        st_s, y_t = _ssm_step_state(l, ssm_state, st_s, xd3, da, bm, cm)
        y_ssm = _ssm_step_gate(y_t, dx, z[0], g_ssm[l])
        x1, h2, comb = _merge(l, o_att, y_ssm[None], gates, ys, ms[2], ms[4], ms[3],
                              g_norm2[l], w_pa, w_pb, w_o, w_router, b_router, tm=db, h2_dtype=F32, prec=HI)
        ys = _moe(l, h2, comb, x1, ms[5], w_gate, w_up, w_down, tm=db, prec=HI)
        outs[5].append(k_new.reshape(db, 1, N_KV_HEADS, HEAD_DIM))
        outs[6].append(v_new.reshape(db, 1, N_KV_HEADS, HEAD_DIM))
        outs[7].append(lf_new.reshape(db, 1, N_HEADS))
        outs[8].append(conv_new)

    stk = [jnp.stack(o) for o in outs]
    return (yp, ys.reshape(db, 1, d), stk[0], stk[1], stk[2], stk[3], stk[4], stk[5], stk[6], stk[7],
            st_s.reshape(depth, db, SSM_HEADS, SSM_HEAD_DIM, D_STATE), jnp.transpose(stk[8], (0, 2, 1, 3)))
```

```python
import functools

import jax
import jax.numpy as jnp
from jax import lax
from jax.experimental import pallas as pl
from jax.experimental.pallas import tpu as pltpu

F32 = jnp.float32
BF16 = jnp.bfloat16
HI = lax.Precision.HIGHEST
EPS = 1e-6
NEG = -1e30

N_HEADS = 16
N_KV_HEADS = 8
HEAD_DIM = 64
ATT_WIDTH = N_HEADS * HEAD_DIM
KV_WIDTH = N_KV_HEADS * HEAD_DIM
SSM_HEADS = 32
SSM_HEAD_DIM = 64
SSM_GROUPS = 4
D_STATE = 128
CONV_WIDTH = 4
D_INNER = SSM_HEADS * SSM_HEAD_DIM
BC_WIDTH = SSM_GROUPS * D_STATE
CONV_DIM = D_INNER + 2 * BC_WIDTH
SSD_CHUNK = 128
N_EXPERTS = 16
N_EXPERT_GROUPS = 4
EXPERTS_PER_GROUP = N_EXPERTS // N_EXPERT_GROUPS

VMEM_LIMIT_BYTES = 56 * 1024 * 1024
LANES = 128

NT_DIMS = (((1,), (1,)), ((), ()))


def _params(*sem):
    return pltpu.CompilerParams(dimension_semantics=sem, vmem_limit_bytes=VMEM_LIMIT_BYTES)


def _sigmoid(x):
    return 1.0 / (1.0 + jnp.exp(-x))


def _silu(x):
    return x * _sigmoid(x)


def _softplus(x):
    return jnp.maximum(x, 0.0) + jnp.log1p(jnp.exp(-jnp.abs(x)))


def _dot(a, b, prec=None):
    return jnp.dot(a, b, precision=prec, preferred_element_type=F32)


def _dot_nt(a, b, prec=None):
    return lax.dot_general(a, b, NT_DIMS, precision=prec, preferred_element_type=F32)


def _eye(n):
    return (lax.broadcasted_iota(jnp.int32, (n, n), 0) == lax.broadcasted_iota(jnp.int32, (n, n), 1)).astype(F32)


def _head_expander(n_heads, width):
    r = lax.broadcasted_iota(jnp.int32, (n_heads, n_heads * width), 0)
    c = lax.broadcasted_iota(jnp.int32, (n_heads, n_heads * width), 1)
    return (c // width == r).astype(F32)


def _mod_kernel(c_ref, w_ref, b_ref, o_ref):
    o_ref[0] = _dot(_silu(c_ref[...]), w_ref[0], HI) + b_ref[0]


def _modulation(c_all, w_mod, b_mod, tn=512):
    depth, d, n = w_mod.shape
    rows = c_all.shape[0]
    return pl.pallas_call(
        _mod_kernel,
        grid=(depth, n // tn),
        in_specs=[pl.BlockSpec((rows, d), lambda l, j: (0, 0)),
                  pl.BlockSpec((1, d, tn), lambda l, j: (l, 0, j)),
                  pl.BlockSpec((1, 1, tn), lambda l, j: (l, 0, j))],
        out_specs=pl.BlockSpec((1, rows, tn), lambda l, j: (l, 0, j)),
        out_shape=jax.ShapeDtypeStruct((depth, rows, n), F32),
        compiler_params=_params("parallel", "parallel"),
        name="modulation",
    )(c_all, w_mod, b_mod.reshape(depth, 1, n))


def _norm_proj_kernel(x_ref, sc_ref, sh_ref, g_ref, w_ref, ws_ref, *refs, starts, prec):
    n_sec = len(starts) - 1
    out_refs, small_ref, h_ref = refs[:n_sec], refs[n_sec], refs[n_sec + 1]
    n = pl.program_id(2)

    @pl.when(n == 0)
    def _():
        x = x_ref[0]
        y = x * lax.rsqrt(jnp.mean(x * x, axis=-1, keepdims=True) + EPS)
        h = (y * g_ref[...]) * (1.0 + sc_ref[0]) + sh_ref[0]
        h = h.astype(h_ref.dtype)
        h_ref[...] = h
        small_ref[0] = _dot(h, ws_ref[...], prec)

    for i in range(n_sec):
        @pl.when(jnp.logical_and(n >= starts[i], n < starts[i + 1]))
        def _(i=i):
            out_refs[i][0] = _dot(h_ref[...], w_ref[...], prec).astype(out_refs[i].dtype)


def _norm_proj(x, sc, sh, g, w_main, w_small, sections, *, tm, tn, prec):
    b, l, d = x.shape
    mod_rows = sc.shape[1]
    mr = tm if mod_rows == l else 1
    starts = [0]
    for nt, _ in sections:
        starts.append(starts[-1] + nt)
    starts = tuple(starts)
    n_tiles = starts[-1]
    assert w_main.shape == (d, n_tiles * tn) and l % tm == 0
    ws_cols = w_small.shape[1]

    def mod_map(bi, mi, ni):
        return (bi, mi if mod_rows == l else 0, 0)

    def sec_map(i):
        lo, cnt = starts[i], sections[i][0]
        return lambda bi, mi, ni: (bi, mi, jnp.clip(ni - lo, 0, cnt - 1))

    out_specs = [pl.BlockSpec((1, tm, tn), sec_map(i)) for i in range(len(sections))]
    out_specs.append(pl.BlockSpec((1, tm, ws_cols), lambda bi, mi, ni: (bi, mi, 0)))
    out_shape = [jax.ShapeDtypeStruct((b, l, nt * tn), dt) for nt, dt in sections]
    out_shape.append(jax.ShapeDtypeStruct((b, l, ws_cols), F32))
    return pl.pallas_call(
        functools.partial(_norm_proj_kernel, starts=starts, prec=prec),
        grid=(b, l // tm, n_tiles),
        in_specs=[pl.BlockSpec((1, tm, d), lambda bi, mi, ni: (bi, mi, 0)),
                  pl.BlockSpec((1, mr, d), mod_map),
                  pl.BlockSpec((1, mr, d), mod_map),
                  pl.BlockSpec((1, d), lambda bi, mi, ni: (0, 0)),
                  pl.BlockSpec((d, tn), lambda bi, mi, ni: (0, ni)),
                  pl.BlockSpec((d, ws_cols), lambda bi, mi, ni: (0, 0))],
        out_specs=out_specs,
        out_shape=out_shape,
        scratch_shapes=[pltpu.VMEM((tm, d), w_main.dtype)],
        compiler_params=_params("parallel", "parallel", "arbitrary"),
        name="norm_proj",
    )(x, sc, sh, g.reshape(1, d), w_main, w_small)


def _head_norm(x, g):
    return x * lax.rsqrt(jnp.mean(x * x, axis=-1, keepdims=True) + EPS) * g


def _log_forget(small_ref, bf_ref):
    fl = small_ref[0][:, :N_HEADS] + bf_ref[...]
    return -_softplus(-fl)


def _qkv_post_prompt_kernel(qkv_ref, small_ref, gq_ref, gk_ref, bf_ref, *refs):
    qh_ref, kh_ref, vh_ref, ko_ref, vo_ref, lf_ref = refs[-6:]
    ko_ref, vo_ref = ko_ref.at[0], vo_ref.at[0]
    scale = HEAD_DIM ** -0.5
    for h in range(N_HEADS):
        q = qkv_ref[0, :, h * HEAD_DIM:(h + 1) * HEAD_DIM]
        qh_ref[0, h] = (_head_norm(q, gq_ref[...]) * scale).astype(qh_ref.dtype)
    for g in range(N_KV_HEADS):
        lo = ATT_WIDTH + g * HEAD_DIM
        kn = _head_norm(qkv_ref[0, :, lo:lo + HEAD_DIM], gk_ref[...])
        kh_ref[0, g] = kn.astype(kh_ref.dtype)
        ko_ref[0, :, g * HEAD_DIM:(g + 1) * HEAD_DIM] = kn
        lo = ATT_WIDTH + KV_WIDTH + g * HEAD_DIM
        v = qkv_ref[0, :, lo:lo + HEAD_DIM].astype(vh_ref.dtype)
        rep = N_HEADS // N_KV_HEADS
        for r in range(rep):
            vh_ref[0, g, :, r * HEAD_DIM:(r + 1) * HEAD_DIM] = v
        vh_ref[0, g, :, rep * HEAD_DIM:] = jnp.ones((v.shape[0], rep * HEAD_DIM), vh_ref.dtype)
    vo_ref[0] = qkv_ref[0, :, ATT_WIDTH + KV_WIDTH:ATT_WIDTH + 2 * KV_WIDTH]
    lf_ref[0] = _log_forget(small_ref, bf_ref)


def _qkv_post_prompt(layer, depth, kv_all, qkv, small, g_q, g_k, b_f, *, tm):
    b, l, _ = qkv.shape
    tok = lambda w: pl.BlockSpec((1, tm, w), lambda bi, mi: (bi, mi, 0))
    head = lambda n, w=HEAD_DIM: pl.BlockSpec((1, n, tm, w), lambda bi, mi: (bi, 0, mi, 0))
    vec = lambda w: pl.BlockSpec((1, w), lambda bi, mi: (0, 0))
    kv_out = pl.BlockSpec((1, 1, tm, KV_WIDTH), lambda bi, mi: (layer, bi, mi, 0))
    v_width = 2 * (N_HEADS // N_KV_HEADS) * HEAD_DIM
    in_specs = [tok(ATT_WIDTH + 2 * KV_WIDTH), tok(LANES), vec(HEAD_DIM), vec(HEAD_DIM), vec(N_HEADS)]
    args = [qkv, small, g_q.reshape(1, -1), g_k.reshape(1, -1), b_f.reshape(1, -1)]
    aliases = {}
    if kv_all is not None:
        in_specs += [pl.BlockSpec(memory_space=pl.ANY)] * 2
        aliases = {len(args): 3, len(args) + 1: 4}
        args += list(kv_all)
    return pl.pallas_call(
        _qkv_post_prompt_kernel,
        grid=(b, l // tm),
        in_specs=in_specs,
        out_specs=[head(N_HEADS), head(N_KV_HEADS), head(N_KV_HEADS, v_width), kv_out, kv_out, tok(N_HEADS)],
        out_shape=[jax.ShapeDtypeStruct((b, N_HEADS, l, HEAD_DIM), BF16),
                   jax.ShapeDtypeStruct((b, N_KV_HEADS, l, HEAD_DIM), BF16),
                   jax.ShapeDtypeStruct((b, N_KV_HEADS, l, v_width), BF16),
                   jax.ShapeDtypeStruct((depth, b, l, KV_WIDTH), F32),
                   jax.ShapeDtypeStruct((depth, b, l, KV_WIDTH), F32),
                   jax.ShapeDtypeStruct((b, l, N_HEADS), F32)],
        input_output_aliases=aliases,
        compiler_params=_params("parallel", "parallel"),
        name="qkv_post_prompt",
    )(*args)


def _qkv_post_sample_kernel(qkv_ref, small_ref, gq_ref, gk_ref, bf_ref, qo_ref, ko_ref, vo_ref, lf_ref):
    scale = HEAD_DIM ** -0.5
    for h in range(N_HEADS):
        sl = slice(h * HEAD_DIM, (h + 1) * HEAD_DIM)
        qo_ref[0, :, sl] = _head_norm(qkv_ref[0, :, sl], gq_ref[...]) * scale
    for g in range(N_KV_HEADS):
        lo = ATT_WIDTH + g * HEAD_DIM
        ko_ref[0, :, g * HEAD_DIM:(g + 1) * HEAD_DIM] = _head_norm(qkv_ref[0, :, lo:lo + HEAD_DIM], gk_ref[...])
    vo_ref[0] = qkv_ref[0, :, ATT_WIDTH + KV_WIDTH:ATT_WIDTH + 2 * KV_WIDTH]
    lf_ref[0] = _log_forget(small_ref, bf_ref)


def _qkv_post_sample(qkv, small, g_q, g_k, b_f):
    b, l, _ = qkv.shape
    tok = lambda w: pl.BlockSpec((1, l, w), lambda bi: (bi, 0, 0))
    vec = lambda w: pl.BlockSpec((1, w), lambda bi: (0, 0))
    return pl.pallas_call(
        _qkv_post_sample_kernel,
        grid=(b,),
        in_specs=[tok(ATT_WIDTH + 2 * KV_WIDTH), tok(LANES), vec(HEAD_DIM), vec(HEAD_DIM), vec(N_HEADS)],
        out_specs=[tok(ATT_WIDTH), tok(KV_WIDTH), tok(KV_WIDTH), tok(N_HEADS)],
        out_shape=[jax.ShapeDtypeStruct((b, l, ATT_WIDTH), F32),
                   jax.ShapeDtypeStruct((b, l, KV_WIDTH), F32),
                   jax.ShapeDtypeStruct((b, l, KV_WIDTH), F32),
                   jax.ShapeDtypeStruct((b, l, N_HEADS), F32)],
        compiler_params=_params("parallel"),
        name="qkv_post_sample",
    )(qkv, small, g_q.reshape(1, -1), g_k.reshape(1, -1), b_f.reshape(1, -1))


def _cumsum_kernel(lf_ref, cf_ref, cft_ref, *, chunk):
    l = lf_ref.shape[1]
    r = lax.broadcasted_iota(jnp.int32, (chunk, chunk), 0)
    c = lax.broadcasted_iota(jnp.int32, (chunk, chunk), 1)
    tril = (r >= c).astype(F32)
    eye = _eye(N_HEADS)
    carry = jnp.zeros((1, N_HEADS), F32)
    for i in range(l // chunk):
        sl = slice(i * chunk, (i + 1) * chunk)
        loc = _dot(tril, lf_ref[0, sl, :], HI) + carry
        cf_ref[0, sl, :] = loc
        cft_ref[0, :, sl] = _dot_nt(eye, loc, HI)
        carry = loc[chunk - 1:chunk, :]


def _cumsum_logf(lf, *, chunk=256):
    b, l, h = lf.shape
    return pl.pallas_call(
        functools.partial(_cumsum_kernel, chunk=chunk),
        grid=(b,),
        in_specs=[pl.BlockSpec((1, l, h), lambda bi: (bi, 0, 0))],
        out_specs=[pl.BlockSpec((1, l, h), lambda bi: (bi, 0, 0)), pl.BlockSpec((1, h, l), lambda bi: (bi, 0, 0))],
        out_shape=[jax.ShapeDtypeStruct((b, l, h), F32), jax.ShapeDtypeStruct((b, h, l), F32)],
        compiler_params=_params("parallel"),
        name="cumsum_logf",
    )(lf)


def _attn_prompt_kernel(q_ref, k_ref, v_ref, cf_ref, cft_ref, o_ref, m_ref, l_ref, acc_ref, *, tq):
    g, qi = pl.program_id(1), pl.program_id(2)
    rep = N_HEADS // N_KV_HEADS
    tk = tq
    width = rep * HEAD_DIM
    lane16 = lax.broadcasted_iota(jnp.int32, (tq, N_HEADS), 1)
    cfb = cf_ref[0]
    cq = [jnp.broadcast_to(jnp.sum(jnp.where(lane16 == rep * g + r, cfb, 0.0), axis=-1, keepdims=True), (tq, width))
          for r in range(rep)]
    head_of_lane = lax.broadcasted_iota(jnp.int32, (tq, width), 1) // HEAD_DIM

    def per_head(vals):
        out = vals[0]
        for r in range(1, rep):
            out = jnp.where(head_of_lane == r, vals[r], out)
        return out

    m_ref[...] = jnp.full(m_ref.shape, NEG, F32)
    l_ref[...] = jnp.zeros(l_ref.shape, F32)
    acc_ref[...] = jnp.zeros(acc_ref.shape, F32)

    def block(ki, diagonal):
        keys = pl.ds(pl.multiple_of(ki * tk, tk), tk)
        alphas, pvs = [], []
        for r in range(rep):
            t = _dot_nt(q_ref[0, r], k_ref[0, 0, keys, :]) - cft_ref[0, 0, ki, r:r + 1, :]
            if diagonal:
                row = lax.broadcasted_iota(jnp.int32, (tq, tk), 0)
                col = lax.broadcasted_iota(jnp.int32, (tq, tk), 1)
                t = jnp.where(col <= row, t, NEG)
            m_prev = m_ref[r]
            m_new = jnp.maximum(m_prev, jnp.max(t, axis=-1, keepdims=True) + cq[r])
            alpha = jnp.exp(m_prev - m_new)
            p = jnp.exp(t + jnp.tile(cq[r] - m_new, (1, tk // width)))
            pv = _dot(p.astype(v_ref.dtype), v_ref[0, 0, keys, :])
            l_ref[r] = alpha * l_ref[r] + pv[:, width:]
            m_ref[r] = m_new
            alphas.append(alpha)
            pvs.append(pv[:, :width])
        acc_ref[...] = per_head(alphas) * acc_ref[...] + per_head(pvs)

    def body(ki, carry):
        block(ki, False)
        return carry

    lax.fori_loop(0, qi, body, 0)
    block(qi, True)
    o_ref[0] = (acc_ref[...] / per_head([l_ref[r] for r in range(rep)])).astype(o_ref.dtype)


def _attention_prompt(qh, kh, vh_ext, cf, cft, *, tq):
    b, _, l, _ = qh.shape
    rep = N_HEADS // N_KV_HEADS
    width = rep * HEAD_DIM
    nq = l // tq
    cft_tiles = cft.reshape(b, N_KV_HEADS, rep, nq, tq).transpose(0, 1, 3, 2, 4)
    return pl.pallas_call(
        functools.partial(_attn_prompt_kernel, tq=tq),
        grid=(b, N_KV_HEADS, nq),
        in_specs=[pl.BlockSpec((1, rep, tq, HEAD_DIM), lambda bi, g, qi: (bi, g, qi, 0)),
                  pl.BlockSpec((1, 1, l, HEAD_DIM), lambda bi, g, qi: (bi, g, 0, 0)),
                  pl.BlockSpec((1, 1, l, 2 * width), lambda bi, g, qi: (bi, g, 0, 0)),
                  pl.BlockSpec((1, tq, N_HEADS), lambda bi, g, qi: (bi, qi, 0)),
                  pl.BlockSpec((1, 1, nq, rep, tq), lambda bi, g, qi: (bi, g, 0, 0, 0))],
        out_specs=pl.BlockSpec((1, tq, width), lambda bi, g, qi: (bi, qi, g)),
        out_shape=jax.ShapeDtypeStruct((b, l, ATT_WIDTH), BF16),
        scratch_shapes=[pltpu.VMEM((rep, tq, width), F32), pltpu.VMEM((rep, tq, width), F32),
                        pltpu.VMEM((tq, width), F32)],
        compiler_params=_params("parallel", "parallel", "arbitrary"),
        name="attention_prompt",
    )(qh, kh, vh_ext, cf, cft_tiles)


def _attn_decode_kernel(pt_ref, qt_ref, q_ref, kn_ref, vnt_ref, lfn_ref, *refs, n_pages, page):
    del pt_ref
    k_refs, v_refs, lf_refs = refs[:n_pages], refs[n_pages:2 * n_pages], refs[2 * n_pages:3 * n_pages]
    ot_ref, qb_ref, s_ref = refs[3 * n_pages:]
    rep = N_HEADS // N_KV_HEADS

    hrow = lax.broadcasted_iota(jnp.int32, (N_HEADS, N_HEADS * page), 0)
    hcol = lax.broadcasted_iota(jnp.int32, (N_HEADS, N_HEADS * page), 1) // page
    q_all = _dot(qt_ref[0], (hrow == hcol).astype(F32), HI)
    for h in range(N_HEADS):
        qb_ref[h] = q_all[:, h * page:(h + 1) * page]

    j = lax.broadcasted_iota(jnp.int32, (page, page), 0)
    t = lax.broadcasted_iota(jnp.int32, (page, page), 1)
    later = (j > t).astype(F32)
    totals = [jnp.sum(lf_refs[p][0, 0], axis=-1, keepdims=True) for p in range(n_pages)]
    carry = lfn_ref[0]
    for p in reversed(range(n_pages)):
        s_ref[:, p * page:(p + 1) * page] = _dot(lf_refs[p][0, 0], later, HI) + carry
        carry = carry + totals[p]

    for p in range(n_pages):
        cols = slice(p * page, (p + 1) * page)
        for g in range(N_KV_HEADS):
            kt = k_refs[p][0, 0, g]
            for r in range(rep):
                h = g * rep + r
                s_ref[h:h + 1, cols] = s_ref[h:h + 1, cols] + jnp.sum(kt * qb_ref[h], axis=0, keepdims=True)

    s = s_ref[...]
    s_new = jnp.sum(q_ref[0] * kn_ref[0], axis=-1, keepdims=True)
    m = jnp.maximum(jnp.max(s, axis=-1, keepdims=True), s_new)
    pmat = jnp.exp(s - m)
    e_new = jnp.exp(s_new - m)
    denom = jnp.sum(pmat, axis=-1, keepdims=True) + e_new
    s_ref[...] = pmat

    lane = lax.broadcasted_iota(jnp.int32, (HEAD_DIM, N_HEADS), 1)
    ot = jnp.zeros((HEAD_DIM, N_HEADS), F32)
    for g in range(N_KV_HEADS):
        acc = [jnp.zeros((HEAD_DIM, page), F32) for _ in range(rep)]
        for p in range(n_pages):
            vt = v_refs[p][0, 0, g]
            for r in range(rep):
                h = g * rep + r
                acc[r] = acc[r] + vt * s_ref[h:h + 1, p * page:(p + 1) * page]
        for r in range(rep):
            ot = jnp.where(lane == g * rep + r, jnp.sum(acc[r], axis=-1, keepdims=True), ot)
    eye = _eye(N_HEADS)
    e_row = jnp.sum(eye * e_new, axis=0, keepdims=True)
    d_row = jnp.sum(eye * denom, axis=0, keepdims=True)
    ot_ref[0] = (ot + vnt_ref[0] * e_row) / d_row


def _attention_decode(layer, page_table, q, k_new, v_new, lf_new, cache_kt, cache_vt, cache_lft):
    db, n_pages = page_table.shape
    page = cache_kt.shape[-1]
    rep = N_HEADS // N_KV_HEADS
    qt = jnp.swapaxes(q, 1, 2)
    kn = jnp.repeat(k_new, rep, axis=1)
    vnt = jnp.swapaxes(jnp.repeat(v_new, rep, axis=1), 1, 2)
    rows = lambda: pl.BlockSpec((1, N_HEADS, HEAD_DIM), lambda bi, pt: (bi, 0, 0))
    cols = lambda: pl.BlockSpec((1, HEAD_DIM, N_HEADS), lambda bi, pt: (bi, 0, 0))
    kv_spec = lambda p: pl.BlockSpec((1, 1, N_KV_HEADS, HEAD_DIM, page),
                                     lambda bi, pt: (layer, pt[bi, p], 0, 0, 0))
    lf_spec = lambda p: pl.BlockSpec((1, 1, N_HEADS, page), lambda bi, pt: (layer, pt[bi, p], 0, 0))
    pages = range(n_pages)
    grid_spec = pltpu.PrefetchScalarGridSpec(
        num_scalar_prefetch=1,
        grid=(db,),
        in_specs=[cols(), rows(), rows(), cols(), pl.BlockSpec((1, N_HEADS, 1), lambda bi, pt: (bi, 0, 0))]
        + [kv_spec(p) for p in pages] + [kv_spec(p) for p in pages] + [lf_spec(p) for p in pages],
        out_specs=cols(),
        scratch_shapes=[pltpu.VMEM((N_HEADS, HEAD_DIM, page), F32), pltpu.VMEM((N_HEADS, n_pages * page), F32)])
    return pl.pallas_call(
        functools.partial(_attn_decode_kernel, n_pages=n_pages, page=page),
        grid_spec=grid_spec,
        out_shape=jax.ShapeDtypeStruct((db, HEAD_DIM, N_HEADS), F32),
        compiler_params=_params("parallel"),
        name="attention_decode",
    )(page_table, qt, q, kn, vnt, lf_new[:, :, None],
      *([cache_kt] * n_pages), *([cache_vt] * n_pages), *([cache_lft] * n_pages))


def _ssd_prompt_kernel(xbc_ref, z_ref, dt_ref, cw_ref, cb_ref, dtb_ref, alog_ref, dskip_ref, gssm_ref,
                       y_ref, st_ref, cv_ref, ext_ref, state_ref, yd_ref):
    c = pl.program_id(1)
    t = SSD_CHUNK
    hist = 8
    hpg = SSM_HEADS // SSM_GROUPS
    gw = hpg * SSM_HEAD_DIM

    @pl.when(c == 0)
    def _():
        ext_ref[0:hist, :] = jnp.zeros((hist, CONV_DIM), F32)
        state_ref[...] = jnp.zeros(state_ref.shape, F32)

    ext_ref[hist:hist + t, :] = xbc_ref[0].astype(F32)
    conv = cb_ref[...] + ext_ref[hist:hist + t, :] * cw_ref[CONV_WIDTH - 1:CONV_WIDTH, :]
    for j in range(1, CONV_WIDTH):
        conv = conv + ext_ref[hist - j:hist - j + t, :] * cw_ref[CONV_WIDTH - 1 - j:CONV_WIDTH - j, :]
    tail = ext_ref[t:t + hist, :]
    cv_ref[0] = tail[hist - (CONV_WIDTH - 1):, :]
    ext_ref[0:hist, :] = tail
    u = _silu(conv)
    xs = u[:, :D_INNER]

    dt = _softplus(dt_ref[0][:, :SSM_HEADS] + dtb_ref[...])
    a = -jnp.exp(alog_ref[...])
    r = lax.broadcasted_iota(jnp.int32, (t, t), 0)
    cidx = lax.broadcasted_iota(jnp.int32, (t, t), 1)
    tri = r >= cidx
    acs = _dot(tri.astype(F32), dt * a, HI)
    acs_t = _dot_nt(_eye(SSM_HEADS), acs, HI)
    expand = _head_expander(SSM_HEADS, SSM_HEAD_DIM).astype(BF16)

    def per_channel(x):
        hi = x.astype(BF16)
        return _dot(hi, expand) + _dot((x - hi.astype(F32)).astype(BF16), expand)

    dt_x = per_channel(dt)
    ea_x = per_channel(jnp.exp(acs))
    dec_x = per_channel(jnp.exp(acs[t - 1:t, :] - acs))
    xd = xs * dt_x
    xdd = (xd * dec_x).astype(BF16)
    xd_b = xd.astype(BF16)
    cdec_x = ea_x[t - 1:t, :]

    for g in range(SSM_GROUPS):
        bg = u[:, D_INNER + g * D_STATE:D_INNER + (g + 1) * D_STATE]
        cg = u[:, D_INNER + BC_WIDTH + g * D_STATE:D_INNER + BC_WIDTH + (g + 1) * D_STATE]
        bg_b, cg_b = bg.astype(BF16), cg.astype(BF16)
        cb = _dot_nt(cg_b, bg_b)
        gs = slice(g * gw, (g + 1) * gw)
        s_prev = state_ref[g]
        y_off = _dot(cg_b, s_prev.astype(BF16)) * ea_x[:, gs]
        state_ref[g] = s_prev * cdec_x[:, gs] + _dot(bg.T.astype(BF16), xdd[:, gs])
        for hh in range(hpg):
            h = g * hpg + hh
            seg = acs[:, h:h + 1] - acs_t[h:h + 1, :]
            lmat = jnp.exp(jnp.where(tri, seg, NEG))
            hs = slice(h * SSM_HEAD_DIM, (h + 1) * SSM_HEAD_DIM)
            yd_ref[:, hs] = _dot((cb * lmat).astype(BF16), xd_b[:, hs])
        yd_ref[:, gs] = yd_ref[:, gs] + y_off

    y = (yd_ref[...] + dskip_ref[...] * xs) * _silu(z_ref[0].astype(F32))
    for g in range(SSM_GROUPS):
        gs = slice(g * gw, (g + 1) * gw)
        yg = y[:, gs]
        yg = yg * lax.rsqrt(jnp.mean(yg * yg, axis=-1, keepdims=True) + EPS)
        y_ref[0, :, gs] = (yg * gssm_ref[:, gs]).astype(y_ref.dtype)

    @pl.when(c == pl.num_programs(1) - 1)
    def _():
        for g in range(SSM_GROUPS):
            st_ref[0, g * gw:(g + 1) * gw, :] = state_ref[g].T


def _ssd_prompt(xbc, z, dtr, conv_w, conv_b, dt_bias, a_log, d_skip, g_ssm):
    b, l, _ = xbc.shape
    t = SSD_CHUNK
    gw = D_INNER // SSM_GROUPS
    vec = lambda w: pl.BlockSpec((1, w), lambda bi, ci: (0, 0))
    y, st, cv = pl.pallas_call(
        _ssd_prompt_kernel,
        grid=(b, l // t),
        in_specs=[pl.BlockSpec((1, t, CONV_DIM), lambda bi, ci: (bi, ci, 0)),
                  pl.BlockSpec((1, t, D_INNER), lambda bi, ci: (bi, ci, 0)),
                  pl.BlockSpec((1, t, LANES), lambda bi, ci: (bi, ci, 1)),
                  pl.BlockSpec((CONV_WIDTH, CONV_DIM), lambda bi, ci: (0, 0)),
                  vec(CONV_DIM), vec(SSM_HEADS), vec(SSM_HEADS), vec(D_INNER), vec(D_INNER)],
        out_specs=[pl.BlockSpec((1, t, D_INNER), lambda bi, ci: (bi, ci, 0)),
                   pl.BlockSpec((1, D_INNER, D_STATE), lambda bi, ci: (bi, 0, 0)),
                   pl.BlockSpec((1, CONV_WIDTH - 1, CONV_DIM), lambda bi, ci: (bi, 0, 0))],
        out_shape=[jax.ShapeDtypeStruct((b, l, D_INNER), BF16),
                   jax.ShapeDtypeStruct((b, D_INNER, D_STATE), F32),
                   jax.ShapeDtypeStruct((b, CONV_WIDTH - 1, CONV_DIM), F32)],
        scratch_shapes=[pltpu.VMEM((t + 8, CONV_DIM), F32),
                        pltpu.VMEM((SSM_GROUPS, D_STATE, gw), F32),
                        pltpu.VMEM((t, D_INNER), F32)],
        compiler_params=_params("parallel", "arbitrary"),
        name="ssd_prompt",
    )(xbc, z, dtr, conv_w, conv_b.reshape(1, -1), dt_bias.reshape(1, -1), a_log.reshape(1, -1),
      jnp.repeat(d_skip, SSM_HEAD_DIM).reshape(1, -1), g_ssm.reshape(1, -1))
    return y, st.reshape(b, SSM_HEADS, SSM_HEAD_DIM, D_STATE), cv


def _ssm_step_prep_kernel(xbc_ref, conv0_ref, dt_ref, cw_ref, cb_ref, dtb_ref, alog_ref, dskip_ref,
                          convn_ref, xd3_ref, da_ref, b_ref, c_ref, dx_ref):
    w = CONV_WIDTH
    db = xbc_ref.shape[0]
    xbc = xbc_ref[...]
    conv = cb_ref[...] + xbc * cw_ref[w - 1:w, :]
    for i in range(w - 1):
        conv = conv + conv0_ref[0, i] * cw_ref[i:i + 1, :]
    for i in range(w - 2):
        convn_ref[i] = conv0_ref[0, i + 1]
    convn_ref[w - 2] = xbc
    u = _silu(conv)
    xs = u[:, :D_INNER]
    b_ref[...] = u[:, D_INNER:D_INNER + BC_WIDTH]
    c_ref[...] = u[:, D_INNER + BC_WIDTH:]
    dt = _softplus(dt_ref[...][:, :SSM_HEADS] + dtb_ref[...])
    da_ref[...] = jnp.exp(dt * -jnp.exp(alog_ref[...]))
    xd = xs * _dot(dt, _head_expander(SSM_HEADS, SSM_HEAD_DIM), HI)
    dx_ref[...] = dskip_ref[...] * xs
    for j in range(D_INNER // LANES):
        sl = slice(j * LANES, (j + 1) * LANES)
        xt = xd[:, sl].T
        hi = xt.astype(BF16)
        rest = xt - hi.astype(F32)
        mid = rest.astype(BF16)
        xd3_ref[sl, 0:db] = hi
        xd3_ref[sl, db:2 * db] = mid
        xd3_ref[sl, 2 * db:3 * db] = (rest - mid.astype(F32)).astype(BF16)


def _ssm_step_prep(layer, xbc, conv_state, dtr, conv_w, conv_b, dt_bias, a_log, d_skip):
    db = xbc.shape[0]
    assert db == LANES
    full = lambda *s: pl.BlockSpec(s, lambda i: tuple(0 for _ in s))
    hist = CONV_WIDTH - 1
    return pl.pallas_call(
        _ssm_step_prep_kernel,
        grid=(1,),
        in_specs=[full(db, CONV_DIM), pl.BlockSpec((1, hist, db, CONV_DIM), lambda i: (layer, 0, 0, 0)),
                  pl.BlockSpec((db, LANES), lambda i: (0, 1)),
                  full(CONV_WIDTH, CONV_DIM), full(1, CONV_DIM), full(1, SSM_HEADS), full(1, SSM_HEADS),
                  full(1, D_INNER)],
        out_specs=[full(hist, db, CONV_DIM), full(D_INNER, 3 * db), full(db, SSM_HEADS), full(db, BC_WIDTH),
                   full(db, BC_WIDTH), full(db, D_INNER)],
        out_shape=[jax.ShapeDtypeStruct((hist, db, CONV_DIM), F32),
                   jax.ShapeDtypeStruct((D_INNER, 3 * db), BF16),
                   jax.ShapeDtypeStruct((db, SSM_HEADS), F32),
                   jax.ShapeDtypeStruct((db, BC_WIDTH), F32),
                   jax.ShapeDtypeStruct((db, BC_WIDTH), F32),
                   jax.ShapeDtypeStruct((db, D_INNER), F32)],
        compiler_params=_params("arbitrary"),
        name="ssm_step_prep",
    )(xbc, conv_state, dtr, conv_w, conv_b.reshape(1, -1), dt_bias.reshape(1, -1), a_log.reshape(1, -1),
      jnp.repeat(d_skip, SSM_HEAD_DIM).reshape(1, -1))


def _ssm_step_state_kernel(da_ref, s_ref, xd3_ref, b_ref, c_ref, *refs):
    so_ref, yt_ref = refs[-2:]
    s_ref, so_ref = s_ref.at[0], so_ref.at[0]
    bi = pl.program_id(0)
    db = xd3_ref.shape[1] // 3
    hpg = SSM_HEADS // SSM_GROUPS

    @pl.when(bi == 0)
    def _():
        yt_ref[...] = jnp.zeros(yt_ref.shape, F32)

    pick = lax.broadcasted_iota(jnp.int32, (3 * db, D_STATE), 0) % db == bi
    xcol = _dot(xd3_ref[...], jnp.where(pick, 1.0, 0.0).astype(BF16))
    lane = lax.broadcasted_iota(jnp.int32, (SSM_HEAD_DIM, db), 1)
    for h in range(SSM_HEADS):
        rows = slice(h * SSM_HEAD_DIM, (h + 1) * SSM_HEAD_DIM)
        ns = slice((h // hpg) * D_STATE, (h // hpg + 1) * D_STATE)
        s_new = s_ref[0, rows, :] * da_ref[bi, h] + xcol[rows, :] * b_ref[0, :, ns]
        so_ref[0, rows, :] = s_new
        y_col = jnp.sum(s_new * c_ref[0, :, ns], axis=-1, keepdims=True)
        yt_ref[rows, :] = jnp.where(lane == bi, y_col, yt_ref[rows, :])


def _ssm_step_state(layer, state_all, new_state_all, xd3, da, bm, cm):
    depth, db, rows, _ = state_all.shape
    seq3 = lambda w: pl.BlockSpec((1, 1, w), lambda bi: (bi, 0, 0))
    st_spec = pl.BlockSpec((1, 1, rows, D_STATE), lambda bi: (layer, bi, 0, 0))
    in_specs = [pl.BlockSpec(memory_space=pltpu.SMEM), st_spec, pl.BlockSpec((rows, 3 * db), lambda bi: (0, 0)),
                seq3(BC_WIDTH), seq3(BC_WIDTH)]
    args = [da, state_all, xd3, bm[:, None, :], cm[:, None, :]]
    aliases = {}
    if new_state_all is not None:
        in_specs.append(pl.BlockSpec(memory_space=pl.ANY))
        args.append(new_state_all)
        aliases = {len(args) - 1: 0}
    return pl.pallas_call(
        _ssm_step_state_kernel,
        grid=(db,),
        in_specs=in_specs,
        out_specs=[st_spec, pl.BlockSpec((rows, db), lambda bi: (0, 0))],
        out_shape=[jax.ShapeDtypeStruct(state_all.shape, F32), jax.ShapeDtypeStruct((rows, db), F32)],
        input_output_aliases=aliases,
        compiler_params=_params("arbitrary"),
        name="ssm_step_state",
    )(*args)


def _ssm_step_gate_kernel(yt_ref, dx_ref, z_ref, gssm_ref, o_ref):
    gw = D_INNER // SSM_GROUPS
    for g in range(SSM_GROUPS):
        gs = slice(g * gw, (g + 1) * gw)
        y = jnp.concatenate([yt_ref[g * gw + j * LANES:g * gw + (j + 1) * LANES, :].T for j in range(gw // LANES)],
                            axis=-1)
        yg = (y + dx_ref[:, gs]) * _silu(z_ref[:, gs])
        o_ref[:, gs] = yg * lax.rsqrt(jnp.mean(yg * yg, axis=-1, keepdims=True) + EPS) * gssm_ref[:, gs]


def _ssm_step_gate(yt, dx, z, g_ssm):
    db = dx.shape[0]
    assert db == LANES
    full = lambda *s: pl.BlockSpec(s, lambda i: tuple(0 for _ in s))
    return pl.pallas_call(
        _ssm_step_gate_kernel,
        grid=(1,),
        in_specs=[full(D_INNER, db), full(db, D_INNER), full(db, D_INNER), full(1, D_INNER)],
        out_specs=full(db, D_INNER),
        out_shape=jax.ShapeDtypeStruct((db, D_INNER), F32),
        compiler_params=_params("arbitrary"),
        name="ssm_step_gate",
    )(yt, dx, z, g_ssm.reshape(1, -1))


def _route(logits):
    e = jnp.exp(logits - jnp.max(logits, axis=-1, keepdims=True))
    probs = e / jnp.sum(e, axis=-1, keepdims=True)
    lane_i = lax.broadcasted_iota(jnp.int32, probs.shape, 1)
    lane = lane_i.astype(F32)
    grp = (lane_i // EXPERTS_PER_GROUP).astype(F32)
    best = jnp.max(jnp.where(grp == 0.0, probs, -1.0), axis=-1, keepdims=True)
    sel = jnp.zeros(best.shape, F32)
    for g in range(1, N_EXPERT_GROUPS):
        gmax = jnp.max(jnp.where(grp == float(g), probs, -1.0), axis=-1, keepdims=True)
        sel = jnp.where(gmax > best, float(g), sel)
        best = jnp.maximum(best, gmax)
    pin = jnp.where(grp == sel, probs, -1.0)
    v1 = jnp.max(pin, axis=-1, keepdims=True)
    i1 = jnp.min(jnp.where(pin == v1, lane, float(N_EXPERTS)), axis=-1, keepdims=True)
    pin2 = jnp.where(lane == i1, -1.0, pin)
    v2 = jnp.max(pin2, axis=-1, keepdims=True)
    i2 = jnp.min(jnp.where(pin2 == v2, lane, float(N_EXPERTS)), axis=-1, keepdims=True)
    tot = v1 + v2
    return jnp.where(lane == i1, v1 / tot, 0.0) + jnp.where(lane == i2, v2 / tot, 0.0)


def _merge_kernel(oa_ref, ys_ref, ga_ref, gb_ref, x_ref, gt_ref, sc_ref, sh_ref, g2_ref,
                  wpa_ref, wpb_ref, wo_ref, wr_ref, br_ref, x1_ref, h2_ref, comb_ref, *, prec):
    cdt = wpa_ref.dtype
    pa = _dot(oa_ref[0].astype(cdt), wpa_ref[0], prec)
    pb = _dot(ys_ref[0].astype(cdt), wpb_ref[0], prec)
    m = _sigmoid(ga_ref[0].astype(F32)) * pa + _sigmoid(gb_ref[0].astype(F32)) * pb
    x1 = x_ref[0] + gt_ref[0] * _dot(m.astype(cdt), wo_ref[0], prec)
    x1_ref[0] = x1
    y = x1 * lax.rsqrt(jnp.mean(x1 * x1, axis=-1, keepdims=True) + EPS)
    h2 = (y * g2_ref[...]) * (1.0 + sc_ref[0]) + sh_ref[0]
    h2_ref[0] = h2.astype(h2_ref.dtype)
    comb_ref[0] = _route(_dot(h2, wr_ref[...], HI) + br_ref[...])


def _merge(layer, o_att, y_ssm, gates, x, gt1, sc2, sh2, g2, w_pa, w_pb, w_o, w_router, b_router, *,
           tm, h2_dtype, prec):
    b, l, d = x.shape
    mod_rows = gt1.shape[1]
    mr = tm if mod_rows == l else 1
    tok = lambda w, j=0: pl.BlockSpec((1, tm, w), lambda bi, mi: (bi, mi, j))
    mod = pl.BlockSpec((1, mr, d), lambda bi, mi: (bi, mi if mod_rows == l else 0, 0))
    full = lambda a: pl.BlockSpec(a.shape, lambda bi, mi: tuple(0 for _ in a.shape))
    per_layer = lambda a: pl.BlockSpec((1,) + a.shape[1:], lambda bi, mi: (layer, 0, 0))
    g2r, brr = g2.reshape(1, d), b_router.reshape(1, -1)
    return pl.pallas_call(
        functools.partial(_merge_kernel, prec=prec),
        grid=(b, l // tm),
        in_specs=[tok(ATT_WIDTH), tok(D_INNER), tok(d, 0), tok(d, 1), tok(d), mod, mod, mod, full(g2r),
                  per_layer(w_pa), per_layer(w_pb), per_layer(w_o), full(w_router), full(brr)],
        out_specs=[tok(d), tok(d), tok(N_EXPERTS)],
        out_shape=[jax.ShapeDtypeStruct((b, l, d), F32),
                   jax.ShapeDtypeStruct((b, l, d), h2_dtype),
                   jax.ShapeDtypeStruct((b, l, N_EXPERTS), F32)],
        compiler_params=_params("parallel", "parallel"),
        name="merge",
    )(o_att, y_ssm, gates, gates, x, gt1, sc2, sh2, g2r, w_pa, w_pb, w_o, w_router, brr)


def _moe_kernel(h_ref, comb_ref, x_ref, gt_ref, wg_ref, wu_ref, wd_ref, o_ref, acc_ref, *, prec):
    e = pl.program_id(2)

    @pl.when(e == 0)
    def _():
        acc_ref[...] = jnp.zeros(acc_ref.shape, F32)

    h = h_ref[0]
    comb = comb_ref[0]
    lane = lax.broadcasted_iota(jnp.int32, comb.shape, 1)
    w_e = jnp.sum(jnp.where(lane == e, comb, 0.0), axis=-1, keepdims=True)
    act = _silu(_dot(h, wg_ref[0, 0], prec)) * _dot(h, wu_ref[0, 0], prec) * w_e
    acc_ref[...] += _dot(act.astype(wd_ref.dtype), wd_ref[0, 0], prec)

    @pl.when(e == pl.num_programs(2) - 1)
    def _():
        o_ref[0] = x_ref[0] + gt_ref[0] * acc_ref[...]


def _moe(layer, h2, comb, x1, gt2, w_gate, w_up, w_down, *, tm, prec):
    b, l, d = x1.shape
    _, ne, _, dff = w_gate.shape
    mod_rows = gt2.shape[1]
    mr = tm if mod_rows == l else 1
    tok = lambda w: pl.BlockSpec((1, tm, w), lambda bi, mi, e: (bi, mi, 0))
    return pl.pallas_call(
        functools.partial(_moe_kernel, prec=prec),
        grid=(b, l // tm, ne),
        in_specs=[tok(d), tok(N_EXPERTS), tok(d),
                  pl.BlockSpec((1, mr, d), lambda bi, mi, e: (bi, mi if mod_rows == l else 0, 0)),
                  pl.BlockSpec((1, 1, d, dff), lambda bi, mi, e: (layer, e, 0, 0)),
                  pl.BlockSpec((1, 1, d, dff), lambda bi, mi, e: (layer, e, 0, 0)),
                  pl.BlockSpec((1, 1, dff, d), lambda bi, mi, e: (layer, e, 0, 0))],
        out_specs=tok(d),
        out_shape=jax.ShapeDtypeStruct((b, l, d), F32),
        scratch_shapes=[pltpu.VMEM((tm, d), F32)],
        compiler_params=_params("parallel", "parallel", "arbitrary"),
        name="moe",
    )(h2, comb, x1, gt2, w_gate, w_up, w_down)


TM_PROJ = 2048
TM_POST = 256
TQ_ATTN = 512
TM_MERGE = 512
TM_MOE = 1024
IN_TILE = 512


def _split_w_in(w):
    o_q, o_k, o_v = 0, ATT_WIDTH, ATT_WIDTH + KV_WIDTH
    o_f = o_v + KV_WIDTH
    o_z = o_f + N_HEADS
    o_x = o_z + D_INNER
    o_dt = o_x + CONV_DIM
    o_ga = o_dt + SSM_HEADS
    d_model = w.shape[0]
    main = jnp.concatenate([w[:, o_q:o_f], w[:, o_z:o_x], w[:, o_x:o_dt], w[:, o_ga:]], axis=1)
    small = jnp.zeros((d_model, 2 * LANES), w.dtype)
    small = small.at[:, :N_HEADS].set(w[:, o_f:o_z]).at[:, LANES:LANES + SSM_HEADS].set(w[:, o_dt:o_ga])
    return main, small


def _sections(rest_dtype):
    qkv = (ATT_WIDTH + 2 * KV_WIDTH) // IN_TILE
    return ((qkv, F32), (D_INNER // IN_TILE, rest_dtype), (CONV_DIM // IN_TILE, rest_dtype),
            (2 * ATT_WIDTH // IN_TILE, rest_dtype))


def kernel(x_prompt, x_sample, cache_k, cache_v, cache_logf, state_ssm, state_conv, page_table, c_prompt, c_sample,
           w_mod, b_mod, g_norm1, g_norm2, w_in, b_f, g_q, g_k, conv_w, conv_b, dt_bias, a_log, d_skip, g_ssm,
           w_pa, w_pb, w_o, w_router, b_router, w_gate, w_up, w_down):
    depth = w_in.shape[0]
    bp, seq, d = x_prompt.shape
    db = x_sample.shape[0]
    assert x_sample.shape[1] == 1

    mod = _modulation(jnp.concatenate([c_prompt, c_sample], axis=0), w_mod, b_mod)
    yp = x_prompt
    ys = x_sample.reshape(1, db, d)
    outs = [[] for _ in range(9)]
    st_s = kv_all = None

    cache_kt = jnp.transpose(cache_k, (0, 1, 3, 4, 2))
    cache_vt = jnp.transpose(cache_v, (0, 1, 3, 4, 2))
    cache_lft = jnp.transpose(cache_logf, (0, 1, 3, 2))
    conv_state = jnp.transpose(state_conv, (0, 2, 1, 3))
    ssm_state = state_ssm.reshape(depth, db, D_INNER, D_STATE)
    bf = lambda w: w.astype(BF16)
    w_pa_b, w_pb_b, w_o_b, w_gate_b, w_up_b, w_down_b = bf(w_pa), bf(w_pb), bf(w_o), bf(w_gate), bf(w_up), bf(w_down)

    for l in range(depth):
        mp = [mod[l, :bp, i * d:(i + 1) * d][:, None, :] for i in range(6)]
        ms = [mod[l, bp:, i * d:(i + 1) * d][None] for i in range(6)]
        w_main, w_small = _split_w_in(w_in[l])

        qkv, z, xbc, gates, small = _norm_proj(
            yp, mp[1], mp[0], g_norm1[l], w_main.astype(BF16), w_small.astype(BF16), _sections(BF16),
            tm=min(TM_PROJ, seq), tn=IN_TILE, prec=None)
        qh, kh, vh, k_all, v_all, lf = _qkv_post_prompt(l, depth, kv_all, qkv, small, g_q[l], g_k[l], b_f[l],
                                                        tm=min(TM_POST, seq))
        kv_all = (k_all, v_all)
        cf, cft = _cumsum_logf(lf)
        o_att = _attention_prompt(qh, kh, vh, cf, cft, tq=min(TQ_ATTN, seq))
        y_ssm, st_p, cv_p = _ssd_prompt(xbc, z, small, conv_w[l], conv_b[l], dt_bias[l], a_log[l], d_skip[l],
                                        g_ssm[l])
        x1, h2, comb = _merge(l, o_att, y_ssm, gates, yp, mp[2], mp[4], mp[3], g_norm2[l],
                              w_pa_b, w_pb_b, w_o_b, w_router, b_router,
                              tm=min(TM_MERGE, seq), h2_dtype=BF16, prec=None)
        yp = _moe(l, h2, comb, x1, mp[5], w_gate_b, w_up_b, w_down_b, tm=min(TM_MOE, seq), prec=None)
        outs[2].append(lf)
        outs[3].append(st_p)
        outs[4].append(cv_p)

        qkv, z, xbc, gates, small = _norm_proj(
            ys, ms[1], ms[0], g_norm1[l], w_main, w_small, _sections(F32), tm=db, tn=IN_TILE, prec=HI)
        qn, k_new, v_new, lf_new = _qkv_post_sample(qkv, small, g_q[l], g_k[l], b_f[l])
        o_att_t = _attention_decode(l, page_table, qn.reshape(db, N_HEADS, HEAD_DIM),
                                    k_new.reshape(db, N_KV_HEADS, HEAD_DIM), v_new.reshape(db, N_KV_HEADS, HEAD_DIM),
                                    lf_new[0], cache_kt, cache_vt, cache_lft)
        o_att = jnp.swapaxes(o_att_t, 1, 2).reshape(1, db, ATT_WIDTH)
        conv_new, xd3, da, bm, cm, dx = _ssm_step_prep(
            l, xbc[0], conv_state, small[0], conv_w[l], conv_b[l], dt_bias[l], a_log[l], d_skip[l])
        st_s, y_t = _ssm_step_state(l, ssm_state, st_s, xd3, da, bm, cm)
        y_ssm = _ssm_step_gate(y_t, dx, z[0], g_ssm[l])
        x1, h2, comb = _merge(l, o_att, y_ssm[None], gates, ys, ms[2], ms[4], ms[3],
                              g_norm2[l], w_pa, w_pb, w_o, w_router, b_router, tm=db, h2_dtype=F32, prec=HI)
        ys = _moe(l, h2, comb, x1, ms[5], w_gate, w_up, w_down, tm=db, prec=HI)
        outs[5].append(k_new.reshape(db, 1, N_KV_HEADS, HEAD_DIM))
        outs[6].append(v_new.reshape(db, 1, N_KV_HEADS, HEAD_DIM))
        outs[7].append(lf_new.reshape(db, 1, N_HEADS))
        outs[8].append(conv_new)

    stk = [jnp.stack(o) if o else None for o in outs]
    k_prompt, v_prompt = (a.reshape(depth, bp, seq, N_KV_HEADS, HEAD_DIM) for a in kv_all)
    return (yp, ys.reshape(db, 1, d), k_prompt, v_prompt, stk[2], stk[3], stk[4], stk[5], stk[6], stk[7],
            st_s.reshape(depth, db, SSM_HEADS, SSM_HEAD_DIM, D_STATE), jnp.transpose(stk[8], (0, 2, 1, 3)))
```

```python
import functools

import jax
import jax.numpy as jnp
from jax import lax
from jax.experimental import pallas as pl
from jax.experimental.pallas import tpu as pltpu

F32 = jnp.float32
BF16 = jnp.bfloat16
HI = lax.Precision.HIGHEST
EPS = 1e-6
NEG = -1e30

N_HEADS = 16
N_KV_HEADS = 8
HEAD_DIM = 64
ATT_WIDTH = N_HEADS * HEAD_DIM
KV_WIDTH = N_KV_HEADS * HEAD_DIM
SSM_HEADS = 32
SSM_HEAD_DIM = 64
SSM_GROUPS = 4
D_STATE = 128
CONV_WIDTH = 4
D_INNER = SSM_HEADS * SSM_HEAD_DIM
BC_WIDTH = SSM_GROUPS * D_STATE
CONV_DIM = D_INNER + 2 * BC_WIDTH
SSD_CHUNK = 128
N_EXPERTS = 16
N_EXPERT_GROUPS = 4
EXPERTS_PER_GROUP = N_EXPERTS // N_EXPERT_GROUPS

VMEM_LIMIT_BYTES = 56 * 1024 * 1024
LANES = 128

NT_DIMS = (((1,), (1,)), ((), ()))


def _params(*sem):
    return pltpu.CompilerParams(dimension_semantics=sem, vmem_limit_bytes=VMEM_LIMIT_BYTES)


def _sigmoid(x):
    return 1.0 / (1.0 + jnp.exp(-x))


def _silu(x):
    return x * _sigmoid(x)


def _softplus(x):
    return jnp.maximum(x, 0.0) + jnp.log1p(jnp.exp(-jnp.abs(x)))


def _dot(a, b, prec=None):
    return jnp.dot(a, b, precision=prec, preferred_element_type=F32)


def _dot_nt(a, b, prec=None):
    return lax.dot_general(a, b, NT_DIMS, precision=prec, preferred_element_type=F32)


def _eye(n):
    return (lax.broadcasted_iota(jnp.int32, (n, n), 0) == lax.broadcasted_iota(jnp.int32, (n, n), 1)).astype(F32)


def _head_expander(n_heads, width):
    r = lax.broadcasted_iota(jnp.int32, (n_heads, n_heads * width), 0)
    c = lax.broadcasted_iota(jnp.int32, (n_heads, n_heads * width), 1)
    return (c // width == r).astype(F32)


def _mod_kernel(c_ref, w_ref, b_ref, o_ref):
    o_ref[0] = _dot(_silu(c_ref[...]), w_ref[0], HI) + b_ref[0]


def _modulation(c_all, w_mod, b_mod, tn=512):
    depth, d, n = w_mod.shape
    rows = c_all.shape[0]
    return pl.pallas_call(
        _mod_kernel,
        grid=(depth, n // tn),
        in_specs=[pl.BlockSpec((rows, d), lambda l, j: (0, 0)),
                  pl.BlockSpec((1, d, tn), lambda l, j: (l, 0, j)),
                  pl.BlockSpec((1, 1, tn), lambda l, j: (l, 0, j))],
        out_specs=pl.BlockSpec((1, rows, tn), lambda l, j: (l, 0, j)),
        out_shape=jax.ShapeDtypeStruct((depth, rows, n), F32),
        compiler_params=_params("parallel", "parallel"),
        name="modulation",
    )(c_all, w_mod, b_mod.reshape(depth, 1, n))


def _norm_proj_kernel(x_ref, sc_ref, sh_ref, g_ref, w_ref, ws_ref, *refs, starts, prec):
    n_sec = len(starts) - 1
    out_refs, small_ref, h_ref = refs[:n_sec], refs[n_sec], refs[n_sec + 1]
    n = pl.program_id(2)

    @pl.when(n == 0)
    def _():
        x = x_ref[0]
        y = x * lax.rsqrt(jnp.mean(x * x, axis=-1, keepdims=True) + EPS)
        h = (y * g_ref[...]) * (1.0 + sc_ref[0]) + sh_ref[0]
        h = h.astype(h_ref.dtype)
        h_ref[...] = h
        small_ref[0] = _dot(h, ws_ref[...], prec)

    for i in range(n_sec):
        @pl.when(jnp.logical_and(n >= starts[i], n < starts[i + 1]))
        def _(i=i):
            out_refs[i][0] = _dot(h_ref[...], w_ref[...], prec).astype(out_refs[i].dtype)


def _norm_proj(x, sc, sh, g, w_main, w_small, sections, *, tm, tn, prec):
    b, l, d = x.shape
    mod_rows = sc.shape[1]
    mr = tm if mod_rows == l else 1
    starts = [0]
    for nt, _ in sections:
        starts.append(starts[-1] + nt)
    starts = tuple(starts)
    n_tiles = starts[-1]
    assert w_main.shape == (d, n_tiles * tn) and l % tm == 0
    ws_cols = w_small.shape[1]

    def mod_map(bi, mi, ni):
        return (bi, mi if mod_rows == l else 0, 0)

    def sec_map(i):
        lo, cnt = starts[i], sections[i][0]
        return lambda bi, mi, ni: (bi, mi, jnp.clip(ni - lo, 0, cnt - 1))

    out_specs = [pl.BlockSpec((1, tm, tn), sec_map(i)) for i in range(len(sections))]
    out_specs.append(pl.BlockSpec((1, tm, ws_cols), lambda bi, mi, ni: (bi, mi, 0)))
    out_shape = [jax.ShapeDtypeStruct((b, l, nt * tn), dt) for nt, dt in sections]
    out_shape.append(jax.ShapeDtypeStruct((b, l, ws_cols), F32))
    return pl.pallas_call(
        functools.partial(_norm_proj_kernel, starts=starts, prec=prec),
        grid=(b, l // tm, n_tiles),
        in_specs=[pl.BlockSpec((1, tm, d), lambda bi, mi, ni: (bi, mi, 0)),
                  pl.BlockSpec((1, mr, d), mod_map),
                  pl.BlockSpec((1, mr, d), mod_map),
                  pl.BlockSpec((1, d), lambda bi, mi, ni: (0, 0)),
                  pl.BlockSpec((d, tn), lambda bi, mi, ni: (0, ni)),
                  pl.BlockSpec((d, ws_cols), lambda bi, mi, ni: (0, 0))],
        out_specs=out_specs,
        out_shape=out_shape,
        scratch_shapes=[pltpu.VMEM((tm, d), w_main.dtype)],
        compiler_params=_params("parallel", "parallel", "arbitrary"),
        name="norm_proj",
    )(x, sc, sh, g.reshape(1, d), w_main, w_small)


def _head_norm(x, g):
    return x * lax.rsqrt(jnp.mean(x * x, axis=-1, keepdims=True) + EPS) * g


def _log_forget(small_ref, bf_ref):
    fl = small_ref[0][:, :N_HEADS] + bf_ref[...]
    return -_softplus(-fl)


def _qkv_post_prompt_kernel(qkv_ref, small_ref, gq_ref, gk_ref, bf_ref, *refs):
    qh_ref, kh_ref, vh_ref, ko_ref, vo_ref, lf_ref = refs[-6:]
    ko_ref, vo_ref = ko_ref.at[0], vo_ref.at[0]
    scale = HEAD_DIM ** -0.5
    for h in range(N_HEADS):
        q = qkv_ref[0, :, h * HEAD_DIM:(h + 1) * HEAD_DIM]
        qh_ref[0, h] = (_head_norm(q, gq_ref[...]) * scale).astype(qh_ref.dtype)
    for g in range(N_KV_HEADS):
        lo = ATT_WIDTH + g * HEAD_DIM
        kn = _head_norm(qkv_ref[0, :, lo:lo + HEAD_DIM], gk_ref[...])
        kh_ref[0, g] = kn.astype(kh_ref.dtype)
        ko_ref[0, :, g * HEAD_DIM:(g + 1) * HEAD_DIM] = kn
        lo = ATT_WIDTH + KV_WIDTH + g * HEAD_DIM
        v = qkv_ref[0, :, lo:lo + HEAD_DIM].astype(vh_ref.dtype)
        rep = N_HEADS // N_KV_HEADS
        for r in range(rep):
            vh_ref[0, g, :, r * HEAD_DIM:(r + 1) * HEAD_DIM] = v
        vh_ref[0, g, :, rep * HEAD_DIM:] = jnp.ones((v.shape[0], rep * HEAD_DIM), vh_ref.dtype)
    vo_ref[0] = qkv_ref[0, :, ATT_WIDTH + KV_WIDTH:ATT_WIDTH + 2 * KV_WIDTH]
    lf_ref[0] = _log_forget(small_ref, bf_ref)


def _qkv_post_prompt(layer, depth, kv_all, qkv, small, g_q, g_k, b_f, *, tm):
    b, l, _ = qkv.shape
    tok = lambda w: pl.BlockSpec((1, tm, w), lambda bi, mi: (bi, mi, 0))
    head = lambda n, w=HEAD_DIM: pl.BlockSpec((1, n, tm, w), lambda bi, mi: (bi, 0, mi, 0))
    vec = lambda w: pl.BlockSpec((1, w), lambda bi, mi: (0, 0))
    kv_out = pl.BlockSpec((1, 1, tm, KV_WIDTH), lambda bi, mi: (layer, bi, mi, 0))
    v_width = 2 * (N_HEADS // N_KV_HEADS) * HEAD_DIM
    in_specs = [tok(ATT_WIDTH + 2 * KV_WIDTH), tok(LANES), vec(HEAD_DIM), vec(HEAD_DIM), vec(N_HEADS)]
    args = [qkv, small, g_q.reshape(1, -1), g_k.reshape(1, -1), b_f.reshape(1, -1)]
    aliases = {}
    if kv_all is not None:
        in_specs += [pl.BlockSpec(memory_space=pl.ANY)] * 2
        aliases = {len(args): 3, len(args) + 1: 4}
        args += list(kv_all)
    return pl.pallas_call(
        _qkv_post_prompt_kernel,
        grid=(b, l // tm),
        in_specs=in_specs,
        out_specs=[head(N_HEADS), head(N_KV_HEADS), head(N_KV_HEADS, v_width), kv_out, kv_out, tok(N_HEADS)],
        out_shape=[jax.ShapeDtypeStruct((b, N_HEADS, l, HEAD_DIM), BF16),
                   jax.ShapeDtypeStruct((b, N_KV_HEADS, l, HEAD_DIM), BF16),
                   jax.ShapeDtypeStruct((b, N_KV_HEADS, l, v_width), BF16),
                   jax.ShapeDtypeStruct((depth, b, l, KV_WIDTH), F32),
                   jax.ShapeDtypeStruct((depth, b, l, KV_WIDTH), F32),
                   jax.ShapeDtypeStruct((b, l, N_HEADS), F32)],
        input_output_aliases=aliases,
        compiler_params=_params("parallel", "parallel"),
        name="qkv_post_prompt",
    )(*args)


def _qkv_post_sample_kernel(qkv_ref, small_ref, gq_ref, gk_ref, bf_ref, qo_ref, ko_ref, vo_ref, lf_ref):
    scale = HEAD_DIM ** -0.5
    for h in range(N_HEADS):
        sl = slice(h * HEAD_DIM, (h + 1) * HEAD_DIM)
        qo_ref[0, :, sl] = _head_norm(qkv_ref[0, :, sl], gq_ref[...]) * scale
    for g in range(N_KV_HEADS):
        lo = ATT_WIDTH + g * HEAD_DIM
        ko_ref[0, :, g * HEAD_DIM:(g + 1) * HEAD_DIM] = _head_norm(qkv_ref[0, :, lo:lo + HEAD_DIM], gk_ref[...])
    vo_ref[0] = qkv_ref[0, :, ATT_WIDTH + KV_WIDTH:ATT_WIDTH + 2 * KV_WIDTH]
    lf_ref[0] = _log_forget(small_ref, bf_ref)


def _qkv_post_sample(qkv, small, g_q, g_k, b_f):
    b, l, _ = qkv.shape
    tok = lambda w: pl.BlockSpec((1, l, w), lambda bi: (bi, 0, 0))
    vec = lambda w: pl.BlockSpec((1, w), lambda bi: (0, 0))
    return pl.pallas_call(
        _qkv_post_sample_kernel,
        grid=(b,),
        in_specs=[tok(ATT_WIDTH + 2 * KV_WIDTH), tok(LANES), vec(HEAD_DIM), vec(HEAD_DIM), vec(N_HEADS)],
        out_specs=[tok(ATT_WIDTH), tok(KV_WIDTH), tok(KV_WIDTH), tok(N_HEADS)],
        out_shape=[jax.ShapeDtypeStruct((b, l, ATT_WIDTH), F32),
                   jax.ShapeDtypeStruct((b, l, KV_WIDTH), F32),
                   jax.ShapeDtypeStruct((b, l, KV_WIDTH), F32),
                   jax.ShapeDtypeStruct((b, l, N_HEADS), F32)],
        compiler_params=_params("parallel"),
        name="qkv_post_sample",
    )(qkv, small, g_q.reshape(1, -1), g_k.reshape(1, -1), b_f.reshape(1, -1))


def _cumsum_kernel(lf_ref, cf_ref, cft_ref, *, chunk):
    l = lf_ref.shape[1]
    r = lax.broadcasted_iota(jnp.int32, (chunk, chunk), 0)
    c = lax.broadcasted_iota(jnp.int32, (chunk, chunk), 1)
    tril = (r >= c).astype(F32)
    eye = _eye(N_HEADS)
    carry = jnp.zeros((1, N_HEADS), F32)
    for i in range(l // chunk):
        sl = slice(i * chunk, (i + 1) * chunk)
        loc = _dot(tril, lf_ref[0, sl, :], HI) + carry
        cf_ref[0, sl, :] = loc
        cft_ref[0, :, sl] = _dot_nt(eye, loc, HI)
        carry = loc[chunk - 1:chunk, :]


def _cumsum_logf(lf, *, chunk=256):
    b, l, h = lf.shape
    return pl.pallas_call(
        functools.partial(_cumsum_kernel, chunk=chunk),
        grid=(b,),
        in_specs=[pl.BlockSpec((1, l, h), lambda bi: (bi, 0, 0))],
        out_specs=[pl.BlockSpec((1, l, h), lambda bi: (bi, 0, 0)), pl.BlockSpec((1, h, l), lambda bi: (bi, 0, 0))],
        out_shape=[jax.ShapeDtypeStruct((b, l, h), F32), jax.ShapeDtypeStruct((b, h, l), F32)],
        compiler_params=_params("parallel"),
        name="cumsum_logf",
    )(lf)


def _attn_prompt_kernel(q_ref, k_ref, v_ref, cf_ref, cft_ref, o_ref, m_ref, l_ref, acc_ref, *, tq):
    g, qi = pl.program_id(1), pl.program_id(2)
    rep = N_HEADS // N_KV_HEADS
    tk = tq
    width = rep * HEAD_DIM
    lane16 = lax.broadcasted_iota(jnp.int32, (tq, N_HEADS), 1)
    cfb = cf_ref[0]
    cq = [jnp.broadcast_to(jnp.sum(jnp.where(lane16 == rep * g + r, cfb, 0.0), axis=-1, keepdims=True), (tq, width))
          for r in range(rep)]
    head_of_lane = lax.broadcasted_iota(jnp.int32, (tq, width), 1) // HEAD_DIM

    def per_head(vals):
        out = vals[0]
        for r in range(1, rep):
            out = jnp.where(head_of_lane == r, vals[r], out)
        return out

    m_ref[...] = jnp.full(m_ref.shape, NEG, F32)
    l_ref[...] = jnp.zeros(l_ref.shape, F32)
    acc_ref[...] = jnp.zeros(acc_ref.shape, F32)

    def block(ki, diagonal):
        keys = pl.ds(pl.multiple_of(ki * tk, tk), tk)
        alphas, pvs = [], []
        for r in range(rep):
            t = _dot_nt(q_ref[0, r], k_ref[0, 0, keys, :]) - cft_ref[0, 0, ki, r:r + 1, :]
            if diagonal:
                row = lax.broadcasted_iota(jnp.int32, (tq, tk), 0)
                col = lax.broadcasted_iota(jnp.int32, (tq, tk), 1)
                t = jnp.where(col <= row, t, NEG)
            m_prev = m_ref[r]
            m_new = jnp.maximum(m_prev, jnp.max(t, axis=-1, keepdims=True) + cq[r])
            alpha = jnp.exp(m_prev - m_new)
            p = jnp.exp(t + jnp.tile(cq[r] - m_new, (1, tk // width)))
            pv = _dot(p.astype(v_ref.dtype), v_ref[0, 0, keys, :])
            l_ref[r] = alpha * l_ref[r] + pv[:, width:]
            m_ref[r] = m_new
            alphas.append(alpha)
            pvs.append(pv[:, :width])
        acc_ref[...] = per_head(alphas) * acc_ref[...] + per_head(pvs)

    def body(ki, carry):
        block(ki, False)
        return carry

    lax.fori_loop(0, qi, body, 0)
    block(qi, True)
    o_ref[0] = (acc_ref[...] / per_head([l_ref[r] for r in range(rep)])).astype(o_ref.dtype)


def _attention_prompt(qh, kh, vh_ext, cf, cft, *, tq):
    b, _, l, _ = qh.shape
    rep = N_HEADS // N_KV_HEADS
    width = rep * HEAD_DIM
    nq = l // tq
    cft_tiles = cft.reshape(b, N_KV_HEADS, rep, nq, tq).transpose(0, 1, 3, 2, 4)
    return pl.pallas_call(
        functools.partial(_attn_prompt_kernel, tq=tq),
        grid=(b, N_KV_HEADS, nq),
        in_specs=[pl.BlockSpec((1, rep, tq, HEAD_DIM), lambda bi, g, qi: (bi, g, qi, 0)),
                  pl.BlockSpec((1, 1, l, HEAD_DIM), lambda bi, g, qi: (bi, g, 0, 0)),
                  pl.BlockSpec((1, 1, l, 2 * width), lambda bi, g, qi: (bi, g, 0, 0)),
                  pl.BlockSpec((1, tq, N_HEADS), lambda bi, g, qi: (bi, qi, 0)),
                  pl.BlockSpec((1, 1, nq, rep, tq), lambda bi, g, qi: (bi, g, 0, 0, 0))],
        out_specs=pl.BlockSpec((1, tq, width), lambda bi, g, qi: (bi, qi, g)),
        out_shape=jax.ShapeDtypeStruct((b, l, ATT_WIDTH), BF16),
        scratch_shapes=[pltpu.VMEM((rep, tq, width), F32), pltpu.VMEM((rep, tq, width), F32),
                        pltpu.VMEM((tq, width), F32)],
        compiler_params=_params("parallel", "parallel", "arbitrary"),
        name="attention_prompt",
    )(qh, kh, vh_ext, cf, cft_tiles)


def _attn_decode_kernel(pt_ref, qt_ref, q_ref, kn_ref, vnt_ref, lfn_ref, *refs, n_pages, page):
    del pt_ref
    k_refs, v_refs, lf_refs = refs[:n_pages], refs[n_pages:2 * n_pages], refs[2 * n_pages:3 * n_pages]
    ot_ref, qb_ref, s_ref = refs[3 * n_pages:]
    rep = N_HEADS // N_KV_HEADS

    hrow = lax.broadcasted_iota(jnp.int32, (N_HEADS, N_HEADS * page), 0)
    hcol = lax.broadcasted_iota(jnp.int32, (N_HEADS, N_HEADS * page), 1) // page
    q_all = _dot(qt_ref[0], (hrow == hcol).astype(F32), HI)
    for h in range(N_HEADS):
        qb_ref[h] = q_all[:, h * page:(h + 1) * page]

    j = lax.broadcasted_iota(jnp.int32, (page, page), 0)
    t = lax.broadcasted_iota(jnp.int32, (page, page), 1)
    later = (j > t).astype(F32)
    totals = [jnp.sum(lf_refs[p][0, 0], axis=-1, keepdims=True) for p in range(n_pages)]
    carry = lfn_ref[0]
    for p in reversed(range(n_pages)):
        s_ref[:, p * page:(p + 1) * page] = _dot(lf_refs[p][0, 0], later, HI) + carry
        carry = carry + totals[p]

    for p in range(n_pages):
        cols = slice(p * page, (p + 1) * page)
        for g in range(N_KV_HEADS):
            kt = k_refs[p][0, 0, g]
            for r in range(rep):
                h = g * rep + r
                s_ref[h:h + 1, cols] = s_ref[h:h + 1, cols] + jnp.sum(kt * qb_ref[h], axis=0, keepdims=True)

    s = s_ref[...]
    s_new = jnp.sum(q_ref[0] * kn_ref[0], axis=-1, keepdims=True)
    m = jnp.maximum(jnp.max(s, axis=-1, keepdims=True), s_new)
    pmat = jnp.exp(s - m)
    e_new = jnp.exp(s_new - m)
    denom = jnp.sum(pmat, axis=-1, keepdims=True) + e_new
    s_ref[...] = pmat

    lane = lax.broadcasted_iota(jnp.int32, (HEAD_DIM, N_HEADS), 1)
    ot = jnp.zeros((HEAD_DIM, N_HEADS), F32)
    for g in range(N_KV_HEADS):
        acc = [jnp.zeros((HEAD_DIM, page), F32) for _ in range(rep)]
        for p in range(n_pages):
            vt = v_refs[p][0, 0, g]
            for r in range(rep):
                h = g * rep + r
                acc[r] = acc[r] + vt * s_ref[h:h + 1, p * page:(p + 1) * page]
        for r in range(rep):
            ot = jnp.where(lane == g * rep + r, jnp.sum(acc[r], axis=-1, keepdims=True), ot)
    eye = _eye(N_HEADS)
    e_row = jnp.sum(eye * e_new, axis=0, keepdims=True)
    d_row = jnp.sum(eye * denom, axis=0, keepdims=True)
    ot_ref[0] = (ot + vnt_ref[0] * e_row) / d_row


def _attention_decode(layer, page_table, q, k_new, v_new, lf_new, cache_kt, cache_vt, cache_lft):
    db, n_pages = page_table.shape
    page = cache_kt.shape[-1]
    rep = N_HEADS // N_KV_HEADS
    qt = jnp.swapaxes(q, 1, 2)
    kn = jnp.repeat(k_new, rep, axis=1)
    vnt = jnp.swapaxes(jnp.repeat(v_new, rep, axis=1), 1, 2)
    rows = lambda: pl.BlockSpec((1, N_HEADS, HEAD_DIM), lambda bi, pt: (bi, 0, 0))
    cols = lambda: pl.BlockSpec((1, HEAD_DIM, N_HEADS), lambda bi, pt: (bi, 0, 0))
    kv_spec = lambda p: pl.BlockSpec((1, 1, N_KV_HEADS, HEAD_DIM, page),
                                     lambda bi, pt: (layer, pt[bi, p], 0, 0, 0))
    lf_spec = lambda p: pl.BlockSpec((1, 1, N_HEADS, page), lambda bi, pt: (layer, pt[bi, p], 0, 0))
    pages = range(n_pages)
    grid_spec = pltpu.PrefetchScalarGridSpec(
        num_scalar_prefetch=1,
        grid=(db,),
        in_specs=[cols(), rows(), rows(), cols(), pl.BlockSpec((1, N_HEADS, 1), lambda bi, pt: (bi, 0, 0))]
        + [kv_spec(p) for p in pages] + [kv_spec(p) for p in pages] + [lf_spec(p) for p in pages],
        out_specs=cols(),
        scratch_shapes=[pltpu.VMEM((N_HEADS, HEAD_DIM, page), F32), pltpu.VMEM((N_HEADS, n_pages * page), F32)])
    return pl.pallas_call(
        functools.partial(_attn_decode_kernel, n_pages=n_pages, page=page),
        grid_spec=grid_spec,
        out_shape=jax.ShapeDtypeStruct((db, HEAD_DIM, N_HEADS), F32),
        compiler_params=_params("parallel"),
        name="attention_decode",
    )(page_table, qt, q, kn, vnt, lf_new[:, :, None],
      *([cache_kt] * n_pages), *([cache_vt] * n_pages), *([cache_lft] * n_pages))


def _ssd_prompt_kernel(xbc_ref, z_ref, dt_ref, cw_ref, cb_ref, dtb_ref, alog_ref, dskip_ref, gssm_ref,
                       y_ref, st_ref, cv_ref, ext_ref, state_ref, yd_ref):
    c = pl.program_id(1)
    t = SSD_CHUNK
    hist = 8
    hpg = SSM_HEADS // SSM_GROUPS
    gw = hpg * SSM_HEAD_DIM

    @pl.when(c == 0)
    def _():
        ext_ref[0:hist, :] = jnp.zeros((hist, CONV_DIM), F32)
        state_ref[...] = jnp.zeros(state_ref.shape, F32)

    ext_ref[hist:hist + t, :] = xbc_ref[0].astype(F32)
    conv = cb_ref[...] + ext_ref[hist:hist + t, :] * cw_ref[CONV_WIDTH - 1:CONV_WIDTH, :]
    for j in range(1, CONV_WIDTH):
        conv = conv + ext_ref[hist - j:hist - j + t, :] * cw_ref[CONV_WIDTH - 1 - j:CONV_WIDTH - j, :]
    tail = ext_ref[t:t + hist, :]
    cv_ref[0] = tail[hist - (CONV_WIDTH - 1):, :]
    ext_ref[0:hist, :] = tail
    u = _silu(conv)
    xs = u[:, :D_INNER]

    dt = _softplus(dt_ref[0][:, :SSM_HEADS] + dtb_ref[...])
    a = -jnp.exp(alog_ref[...])
    r = lax.broadcasted_iota(jnp.int32, (t, t), 0)
    cidx = lax.broadcasted_iota(jnp.int32, (t, t), 1)
    tri = r >= cidx
    acs = _dot(tri.astype(F32), dt * a, HI)
    acs_t = _dot_nt(_eye(SSM_HEADS), acs, HI)
    expand = _head_expander(SSM_HEADS, SSM_HEAD_DIM).astype(BF16)

    def per_channel(x):
        hi = x.astype(BF16)
        return _dot(hi, expand) + _dot((x - hi.astype(F32)).astype(BF16), expand)

    dt_x = per_channel(dt)
    ea_x = per_channel(jnp.exp(acs))
    dec_x = per_channel(jnp.exp(acs[t - 1:t, :] - acs))
    xd = xs * dt_x
    xdd = (xd * dec_x).astype(BF16)
    xd_b = xd.astype(BF16)
    cdec_x = ea_x[t - 1:t, :]

    for g in range(SSM_GROUPS):
        bg = u[:, D_INNER + g * D_STATE:D_INNER + (g + 1) * D_STATE]
        cg = u[:, D_INNER + BC_WIDTH + g * D_STATE:D_INNER + BC_WIDTH + (g + 1) * D_STATE]
        bg_b, cg_b = bg.astype(BF16), cg.astype(BF16)
        cb = _dot_nt(cg_b, bg_b)
        gs = slice(g * gw, (g + 1) * gw)
        s_prev = state_ref[g]
        y_off = _dot(cg_b, s_prev.astype(BF16)) * ea_x[:, gs]
        state_ref[g] = s_prev * cdec_x[:, gs] + _dot(bg.T.astype(BF16), xdd[:, gs])
        for hh in range(hpg):
            h = g * hpg + hh
            seg = acs[:, h:h + 1] - acs_t[h:h + 1, :]
            lmat = jnp.exp(jnp.where(tri, seg, NEG))
            hs = slice(h * SSM_HEAD_DIM, (h + 1) * SSM_HEAD_DIM)
            yd_ref[:, hs] = _dot((cb * lmat).astype(BF16), xd_b[:, hs])
        yd_ref[:, gs] = yd_ref[:, gs] + y_off

    y = (yd_ref[...] + dskip_ref[...] * xs) * _silu(z_ref[0].astype(F32))
    for g in range(SSM_GROUPS):
        gs = slice(g * gw, (g + 1) * gw)
        yg = y[:, gs]
        yg = yg * lax.rsqrt(jnp.mean(yg * yg, axis=-1, keepdims=True) + EPS)
        y_ref[0, :, gs] = (yg * gssm_ref[:, gs]).astype(y_ref.dtype)

    @pl.when(c == pl.num_programs(1) - 1)
    def _():
        for g in range(SSM_GROUPS):
            st_ref[0, g * gw:(g + 1) * gw, :] = state_ref[g].T


def _ssd_prompt(xbc, z, dtr, conv_w, conv_b, dt_bias, a_log, d_skip, g_ssm):
    b, l, _ = xbc.shape
    t = SSD_CHUNK
    gw = D_INNER // SSM_GROUPS
    vec = lambda w: pl.BlockSpec((1, w), lambda bi, ci: (0, 0))
    y, st, cv = pl.pallas_call(
        _ssd_prompt_kernel,
        grid=(b, l // t),
        in_specs=[pl.BlockSpec((1, t, CONV_DIM), lambda bi, ci: (bi, ci, 0)),
                  pl.BlockSpec((1, t, D_INNER), lambda bi, ci: (bi, ci, 0)),
                  pl.BlockSpec((1, t, LANES), lambda bi, ci: (bi, ci, 1)),
                  pl.BlockSpec((CONV_WIDTH, CONV_DIM), lambda bi, ci: (0, 0)),
                  vec(CONV_DIM), vec(SSM_HEADS), vec(SSM_HEADS), vec(D_INNER), vec(D_INNER)],
        out_specs=[pl.BlockSpec((1, t, D_INNER), lambda bi, ci: (bi, ci, 0)),
                   pl.BlockSpec((1, D_INNER, D_STATE), lambda bi, ci: (bi, 0, 0)),
                   pl.BlockSpec((1, CONV_WIDTH - 1, CONV_DIM), lambda bi, ci: (bi, 0, 0))],
        out_shape=[jax.ShapeDtypeStruct((b, l, D_INNER), BF16),
                   jax.ShapeDtypeStruct((b, D_INNER, D_STATE), F32),
                   jax.ShapeDtypeStruct((b, CONV_WIDTH - 1, CONV_DIM), F32)],
        scratch_shapes=[pltpu.VMEM((t + 8, CONV_DIM), F32),
                        pltpu.VMEM((SSM_GROUPS, D_STATE, gw), F32),
                        pltpu.VMEM((t, D_INNER), F32)],
        compiler_params=_params("parallel", "arbitrary"),
        name="ssd_prompt",
    )(xbc, z, dtr, conv_w, conv_b.reshape(1, -1), dt_bias.reshape(1, -1), a_log.reshape(1, -1),
      jnp.repeat(d_skip, SSM_HEAD_DIM).reshape(1, -1), g_ssm.reshape(1, -1))
    return y, st.reshape(b, SSM_HEADS, SSM_HEAD_DIM, D_STATE), cv


def _ssm_step_prep_kernel(xbc_ref, conv0_ref, dt_ref, cw_ref, cb_ref, dtb_ref, alog_ref, dskip_ref,
                          convn_ref, xd3_ref, da_ref, b_ref, c_ref, dx_ref):
    w = CONV_WIDTH
    db = xbc_ref.shape[0]
    xbc = xbc_ref[...]
    conv = cb_ref[...] + xbc * cw_ref[w - 1:w, :]
    for i in range(w - 1):
        conv = conv + conv0_ref[0, i] * cw_ref[i:i + 1, :]
    for i in range(w - 2):
        convn_ref[i] = conv0_ref[0, i + 1]
    convn_ref[w - 2] = xbc
    u = _silu(conv)
    xs = u[:, :D_INNER]
    b_ref[...] = u[:, D_INNER:D_INNER + BC_WIDTH]
    c_ref[...] = u[:, D_INNER + BC_WIDTH:]
    dt = _softplus(dt_ref[...][:, :SSM_HEADS] + dtb_ref[...])
    da_ref[...] = jnp.exp(dt * -jnp.exp(alog_ref[...]))
    xd = xs * _dot(dt, _head_expander(SSM_HEADS, SSM_HEAD_DIM), HI)
    dx_ref[...] = dskip_ref[...] * xs
    for j in range(D_INNER // LANES):
        sl = slice(j * LANES, (j + 1) * LANES)
        xt = xd[:, sl].T
        hi = xt.astype(BF16)
        rest = xt - hi.astype(F32)
        mid = rest.astype(BF16)
        xd3_ref[sl, 0:db] = hi
        xd3_ref[sl, db:2 * db] = mid
        xd3_ref[sl, 2 * db:3 * db] = (rest - mid.astype(F32)).astype(BF16)


def _ssm_step_prep(layer, xbc, conv_state, dtr, conv_w, conv_b, dt_bias, a_log, d_skip):
    db = xbc.shape[0]
    assert db == LANES
    full = lambda *s: pl.BlockSpec(s, lambda i: tuple(0 for _ in s))
    hist = CONV_WIDTH - 1
    return pl.pallas_call(
        _ssm_step_prep_kernel,
        grid=(1,),
        in_specs=[full(db, CONV_DIM), pl.BlockSpec((1, hist, db, CONV_DIM), lambda i: (layer, 0, 0, 0)),
                  pl.BlockSpec((db, LANES), lambda i: (0, 1)),
                  full(CONV_WIDTH, CONV_DIM), full(1, CONV_DIM), full(1, SSM_HEADS), full(1, SSM_HEADS),
                  full(1, D_INNER)],
        out_specs=[full(hist, db, CONV_DIM), full(D_INNER, 3 * db), full(db, SSM_HEADS), full(db, BC_WIDTH),
                   full(db, BC_WIDTH), full(db, D_INNER)],
        out_shape=[jax.ShapeDtypeStruct((hist, db, CONV_DIM), F32),
                   jax.ShapeDtypeStruct((D_INNER, 3 * db), BF16),
                   jax.ShapeDtypeStruct((db, SSM_HEADS), F32),
                   jax.ShapeDtypeStruct((db, BC_WIDTH), F32),
                   jax.ShapeDtypeStruct((db, BC_WIDTH), F32),
                   jax.ShapeDtypeStruct((db, D_INNER), F32)],
        compiler_params=_params("arbitrary"),
        name="ssm_step_prep",
    )(xbc, conv_state, dtr, conv_w, conv_b.reshape(1, -1), dt_bias.reshape(1, -1), a_log.reshape(1, -1),
      jnp.repeat(d_skip, SSM_HEAD_DIM).reshape(1, -1))


def _ssm_step_state_kernel(da_ref, s_ref, xd3_ref, b_ref, c_ref, *refs):
    so_ref, yt_ref = refs[-2:]
    s_ref, so_ref = s_ref.at[0], so_ref.at[0]
    bi = pl.program_id(0)
    db = xd3_ref.shape[1] // 3
    hpg = SSM_HEADS // SSM_GROUPS

    @pl.when(bi == 0)
    def _():
        yt_ref[...] = jnp.zeros(yt_ref.shape, F32)

    pick = lax.broadcasted_iota(jnp.int32, (3 * db, D_STATE), 0) % db == bi
    xcol = _dot(xd3_ref[...], jnp.where(pick, 1.0, 0.0).astype(BF16))
    lane = lax.broadcasted_iota(jnp.int32, (SSM_HEAD_DIM, db), 1)
    for h in range(SSM_HEADS):
        rows = slice(h * SSM_HEAD_DIM, (h + 1) * SSM_HEAD_DIM)
        ns = slice((h // hpg) * D_STATE, (h // hpg + 1) * D_STATE)
        s_new = s_ref[0, rows, :] * da_ref[bi, h] + xcol[rows, :] * b_ref[0, :, ns]
        so_ref[0, rows, :] = s_new
        y_col = jnp.sum(s_new * c_ref[0, :, ns], axis=-1, keepdims=True)
        yt_ref[rows, :] = jnp.where(lane == bi, y_col, yt_ref[rows, :])


def _ssm_step_state(layer, state_all, new_state_all, xd3, da, bm, cm):
    depth, db, rows, _ = state_all.shape
    seq3 = lambda w: pl.BlockSpec((1, 1, w), lambda bi: (bi, 0, 0))
    st_spec = pl.BlockSpec((1, 1, rows, D_STATE), lambda bi: (layer, bi, 0, 0))
    in_specs = [pl.BlockSpec(memory_space=pltpu.SMEM), st_spec, pl.BlockSpec((rows, 3 * db), lambda bi: (0, 0)),
                seq3(BC_WIDTH), seq3(BC_WIDTH)]
    args = [da, state_all, xd3, bm[:, None, :], cm[:, None, :]]
    aliases = {}
    if new_state_all is not None:
        in_specs.append(pl.BlockSpec(memory_space=pl.ANY))
        args.append(new_state_all)
        aliases = {len(args) - 1: 0}
    return pl.pallas_call(
        _ssm_step_state_kernel,
        grid=(db,),
        in_specs=in_specs,
        out_specs=[st_spec, pl.BlockSpec((rows, db), lambda bi: (0, 0))],
        out_shape=[jax.ShapeDtypeStruct(state_all.shape, F32), jax.ShapeDtypeStruct((rows, db), F32)],
        input_output_aliases=aliases,
        compiler_params=_params("arbitrary"),
        name="ssm_step_state",
    )(*args)


def _ssm_step_gate_kernel(yt_ref, dx_ref, z_ref, gssm_ref, o_ref):
    gw = D_INNER // SSM_GROUPS
    for g in range(SSM_GROUPS):
        gs = slice(g * gw, (g + 1) * gw)
        y = jnp.concatenate([yt_ref[g * gw + j * LANES:g * gw + (j + 1) * LANES, :].T for j in range(gw // LANES)],
                            axis=-1)
        yg = (y + dx_ref[:, gs]) * _silu(z_ref[:, gs])
        o_ref[:, gs] = yg * lax.rsqrt(jnp.mean(yg * yg, axis=-1, keepdims=True) + EPS) * gssm_ref[:, gs]


def _ssm_step_gate(yt, dx, z, g_ssm):
    db = dx.shape[0]
    assert db == LANES
    full = lambda *s: pl.BlockSpec(s, lambda i: tuple(0 for _ in s))
    return pl.pallas_call(
        _ssm_step_gate_kernel,
        grid=(1,),
        in_specs=[full(D_INNER, db), full(db, D_INNER), full(db, D_INNER), full(1, D_INNER)],
        out_specs=full(db, D_INNER),
        out_shape=jax.ShapeDtypeStruct((db, D_INNER), F32),
        compiler_params=_params("arbitrary"),
        name="ssm_step_gate",
    )(yt, dx, z, g_ssm.reshape(1, -1))


def _route(logits):
    e = jnp.exp(logits - jnp.max(logits, axis=-1, keepdims=True))
    probs = e / jnp.sum(e, axis=-1, keepdims=True)
    lane_i = lax.broadcasted_iota(jnp.int32, probs.shape, 1)
    lane = lane_i.astype(F32)
    grp = (lane_i // EXPERTS_PER_GROUP).astype(F32)
    best = jnp.max(jnp.where(grp == 0.0, probs, -1.0), axis=-1, keepdims=True)
    sel = jnp.zeros(best.shape, F32)
    for g in range(1, N_EXPERT_GROUPS):
        gmax = jnp.max(jnp.where(grp == float(g), probs, -1.0), axis=-1, keepdims=True)
        sel = jnp.where(gmax > best, float(g), sel)
        best = jnp.maximum(best, gmax)
    pin = jnp.where(grp == sel, probs, -1.0)
    v1 = jnp.max(pin, axis=-1, keepdims=True)
    i1 = jnp.min(jnp.where(pin == v1, lane, float(N_EXPERTS)), axis=-1, keepdims=True)
    pin2 = jnp.where(lane == i1, -1.0, pin)
    v2 = jnp.max(pin2, axis=-1, keepdims=True)
    i2 = jnp.min(jnp.where(pin2 == v2, lane, float(N_EXPERTS)), axis=-1, keepdims=True)
    tot = v1 + v2
    return jnp.where(lane == i1, v1 / tot, 0.0) + jnp.where(lane == i2, v2 / tot, 0.0)


MERGE_ROW_PARTS = 2
MERGE_MIN_PART_ROWS = 256


def _merge_kernel(oa_ref, ys_ref, ga_ref, gb_ref, x_ref, gt_ref, sc_ref, sh_ref, g2_ref,
                  wpa_ref, wpb_ref, wo_ref, wr_ref, br_ref, x1_ref, h2_ref, comb_ref, *, prec):
    cdt = wpa_ref.dtype
    tm = x_ref.shape[1]
    n_parts = MERGE_ROW_PARTS if tm >= MERGE_ROW_PARTS * MERGE_MIN_PART_ROWS else 1
    part = tm // n_parts
    for i in range(n_parts):
        rows = slice(i * part, (i + 1) * part)
        mrows = rows if gt_ref.shape[1] == tm else slice(None)
        pa = _dot(oa_ref[0, rows, :].astype(cdt), wpa_ref[0], prec)
        pb = _dot(ys_ref[0, rows, :].astype(cdt), wpb_ref[0], prec)
        m = _sigmoid(ga_ref[0, rows, :].astype(F32)) * pa + _sigmoid(gb_ref[0, rows, :].astype(F32)) * pb
        x1 = x_ref[0, rows, :] + gt_ref[0, mrows, :] * _dot(m.astype(cdt), wo_ref[0], prec)
        x1_ref[0, rows, :] = x1
        y = x1 * lax.rsqrt(jnp.mean(x1 * x1, axis=-1, keepdims=True) + EPS)
        h2 = (y * g2_ref[...]) * (1.0 + sc_ref[0, mrows, :]) + sh_ref[0, mrows, :]
        h2_ref[0, rows, :] = h2.astype(h2_ref.dtype)
        comb_ref[0, rows, :] = _route(_dot(h2, wr_ref[...], HI) + br_ref[...])


def _merge(layer, o_att, y_ssm, gates, x, gt1, sc2, sh2, g2, w_pa, w_pb, w_o, w_router, b_router, *,
           tm, h2_dtype, prec):
    b, l, d = x.shape
    mod_rows = gt1.shape[1]
    mr = tm if mod_rows == l else 1
    tok = lambda w, j=0: pl.BlockSpec((1, tm, w), lambda bi, mi: (bi, mi, j))
    mod = pl.BlockSpec((1, mr, d), lambda bi, mi: (bi, mi if mod_rows == l else 0, 0))
    full = lambda a: pl.BlockSpec(a.shape, lambda bi, mi: tuple(0 for _ in a.shape))
    per_layer = lambda a: pl.BlockSpec((1,) + a.shape[1:], lambda bi, mi: (layer, 0, 0))
    g2r, brr = g2.reshape(1, d), b_router.reshape(1, -1)
    return pl.pallas_call(
        functools.partial(_merge_kernel, prec=prec),
        grid=(b, l // tm),
        in_specs=[tok(ATT_WIDTH), tok(D_INNER), tok(d, 0), tok(d, 1), tok(d), mod, mod, mod, full(g2r),
                  per_layer(w_pa), per_layer(w_pb), per_layer(w_o), full(w_router), full(brr)],
        out_specs=[tok(d), tok(d), tok(N_EXPERTS)],
        out_shape=[jax.ShapeDtypeStruct((b, l, d), F32),
                   jax.ShapeDtypeStruct((b, l, d), h2_dtype),
                   jax.ShapeDtypeStruct((b, l, N_EXPERTS), F32)],
        compiler_params=_params("parallel", "parallel"),
        name="merge",
    )(o_att, y_ssm, gates, gates, x, gt1, sc2, sh2, g2r, w_pa, w_pb, w_o, w_router, brr)


def _moe_kernel(h_ref, comb_ref, x_ref, gt_ref, wg_ref, wu_ref, wd_ref, o_ref, acc_ref, *, prec):
    e = pl.program_id(2)

    @pl.when(e == 0)
    def _():
        acc_ref[...] = jnp.zeros(acc_ref.shape, F32)

    h = h_ref[0]
    comb = comb_ref[0]
    lane = lax.broadcasted_iota(jnp.int32, comb.shape, 1)
    per_step = wg_ref.shape[1]
    acts = []
    for j in range(per_step):
        w_e = jnp.sum(jnp.where(lane == e * per_step + j, comb, 0.0), axis=-1, keepdims=True)
        act = _silu(_dot(h, wg_ref[0, j], prec)) * _dot(h, wu_ref[0, j], prec) * w_e
        acts.append(act.astype(wd_ref.dtype))
    dff, d = wd_ref.shape[2], wd_ref.shape[3]
    acc_ref[...] += _dot(jnp.concatenate(acts, axis=-1), wd_ref[0].reshape(per_step * dff, d), prec)

    @pl.when(e == pl.num_programs(2) - 1)
    def _():
        o_ref[0] = x_ref[0] + gt_ref[0] * acc_ref[...]


def _moe(layer, h2, comb, x1, gt2, w_gate, w_up, w_down, *, tm, prec):
    b, l, d = x1.shape
    _, ne, _, dff = w_gate.shape
    mod_rows = gt2.shape[1]
    mr = tm if mod_rows == l else 1
    tok = lambda w: pl.BlockSpec((1, tm, w), lambda bi, mi, e: (bi, mi, 0))
    eps = MOE_EXPERTS_PER_STEP
    return pl.pallas_call(
        functools.partial(_moe_kernel, prec=prec),
        grid=(b, l // tm, ne // eps),
        in_specs=[tok(d), tok(N_EXPERTS), tok(d),
                  pl.BlockSpec((1, mr, d), lambda bi, mi, e: (bi, mi if mod_rows == l else 0, 0)),
                  pl.BlockSpec((1, eps, d, dff), lambda bi, mi, e: (layer, e, 0, 0)),
                  pl.BlockSpec((1, eps, d, dff), lambda bi, mi, e: (layer, e, 0, 0)),
                  pl.BlockSpec((1, eps, dff, d), lambda bi, mi, e: (layer, e, 0, 0))],
        out_specs=tok(d),
        out_shape=jax.ShapeDtypeStruct((b, l, d), F32),
        scratch_shapes=[pltpu.VMEM((tm, d), F32)],
        compiler_params=_params("parallel", "parallel", "arbitrary"),
        name="moe",
    )(h2, comb, x1, gt2, w_gate, w_up, w_down)


TM_PROJ = 2048
TM_POST = 256
TQ_ATTN = 512
TM_MERGE = 512
TM_MOE = 1024
MOE_EXPERTS_PER_STEP = 2
IN_TILE = 512


def _split_w_in(w):
    o_q, o_k, o_v = 0, ATT_WIDTH, ATT_WIDTH + KV_WIDTH
    o_f = o_v + KV_WIDTH
    o_z = o_f + N_HEADS
    o_x = o_z + D_INNER
    o_dt = o_x + CONV_DIM
    o_ga = o_dt + SSM_HEADS
    d_model = w.shape[0]
    main = jnp.concatenate([w[:, o_q:o_f], w[:, o_z:o_x], w[:, o_x:o_dt], w[:, o_ga:]], axis=1)
    small = jnp.zeros((d_model, 2 * LANES), w.dtype)
    small = small.at[:, :N_HEADS].set(w[:, o_f:o_z]).at[:, LANES:LANES + SSM_HEADS].set(w[:, o_dt:o_ga])
    return main, small


def _sections(rest_dtype):
    qkv = (ATT_WIDTH + 2 * KV_WIDTH) // IN_TILE
    return ((qkv, F32), (D_INNER // IN_TILE, rest_dtype), (CONV_DIM // IN_TILE, rest_dtype),
            (2 * ATT_WIDTH // IN_TILE, rest_dtype))


def kernel(x_prompt, x_sample, cache_k, cache_v, cache_logf, state_ssm, state_conv, page_table, c_prompt, c_sample,
           w_mod, b_mod, g_norm1, g_norm2, w_in, b_f, g_q, g_k, conv_w, conv_b, dt_bias, a_log, d_skip, g_ssm,
           w_pa, w_pb, w_o, w_router, b_router, w_gate, w_up, w_down):
    depth = w_in.shape[0]
    bp, seq, d = x_prompt.shape
    db = x_sample.shape[0]
    assert x_sample.shape[1] == 1

    mod = _modulation(jnp.concatenate([c_prompt, c_sample], axis=0), w_mod, b_mod)
    yp = x_prompt
    ys = x_sample.reshape(1, db, d)
    outs = [[] for _ in range(9)]
    st_s = kv_all = None

    cache_kt = jnp.transpose(cache_k, (0, 1, 3, 4, 2))
    cache_vt = jnp.transpose(cache_v, (0, 1, 3, 4, 2))
    cache_lft = jnp.transpose(cache_logf, (0, 1, 3, 2))
    conv_state = jnp.transpose(state_conv, (0, 2, 1, 3))
    ssm_state = state_ssm.reshape(depth, db, D_INNER, D_STATE)
    bf = lambda w: w.astype(BF16)
    w_pa_b, w_pb_b, w_o_b, w_gate_b, w_up_b, w_down_b = bf(w_pa), bf(w_pb), bf(w_o), bf(w_gate), bf(w_up), bf(w_down)

    for l in range(depth):
        mp = [mod[l, :bp, i * d:(i + 1) * d][:, None, :] for i in range(6)]
        ms = [mod[l, bp:, i * d:(i + 1) * d][None] for i in range(6)]
        w_main, w_small = _split_w_in(w_in[l])

        qkv, z, xbc, gates, small = _norm_proj(
            yp, mp[1], mp[0], g_norm1[l], w_main.astype(BF16), w_small.astype(BF16), _sections(BF16),
            tm=min(TM_PROJ, seq), tn=IN_TILE, prec=None)
        qh, kh, vh, k_all, v_all, lf = _qkv_post_prompt(l, depth, kv_all, qkv, small, g_q[l], g_k[l], b_f[l],
                                                        tm=min(TM_POST, seq))
        kv_all = (k_all, v_all)
        cf, cft = _cumsum_logf(lf)
        o_att = _attention_prompt(qh, kh, vh, cf, cft, tq=min(TQ_ATTN, seq))
        y_ssm, st_p, cv_p = _ssd_prompt(xbc, z, small, conv_w[l], conv_b[l], dt_bias[l], a_log[l], d_skip[l],
                                        g_ssm[l])
        x1, h2, comb = _merge(l, o_att, y_ssm, gates, yp, mp[2], mp[4], mp[3], g_norm2[l],
                              w_pa_b, w_pb_b, w_o_b, w_router, b_router,
                              tm=min(TM_MERGE, seq), h2_dtype=BF16, prec=None)
        yp = _moe(l, h2, comb, x1, mp[5], w_gate_b, w_up_b, w_down_b, tm=min(TM_MOE, seq), prec=None)
        outs[2].append(lf)
        outs[3].append(st_p)
        outs[4].append(cv_p)

        qkv, z, xbc, gates, small = _norm_proj(
            ys, ms[1], ms[0], g_norm1[l], w_main, w_small, _sections(F32), tm=db, tn=IN_TILE, prec=HI)
        qn, k_new, v_new, lf_new = _qkv_post_sample(qkv, small, g_q[l], g_k[l], b_f[l])
        o_att_t = _attention_decode(l, page_table, qn.reshape(db, N_HEADS, HEAD_DIM),
                                    k_new.reshape(db, N_KV_HEADS, HEAD_DIM), v_new.reshape(db, N_KV_HEADS, HEAD_DIM),
                                    lf_new[0], cache_kt, cache_vt, cache_lft)
        o_att = jnp.swapaxes(o_att_t, 1, 2).reshape(1, db, ATT_WIDTH)
        conv_new, xd3, da, bm, cm, dx = _ssm_step_prep(
            l, xbc[0], conv_state, small[0], conv_w[l], conv_b[l], dt_bias[l], a_log[l], d_skip[l])
        st_s, y_t = _ssm_step_state(l, ssm_state, st_s, xd3, da, bm, cm)
        y_ssm = _ssm_step_gate(y_t, dx, z[0], g_ssm[l])
        x1, h2, comb = _merge(l, o_att, y_ssm[None], gates, ys, ms[2], ms[4], ms[3],
                              g_norm2[l], w_pa, w_pb, w_o, w_router, b_router, tm=db, h2_dtype=F32, prec=HI)
        ys = _moe(l, h2, comb, x1, ms[5], w_gate, w_up, w_down, tm=db, prec=HI)
        outs[5].append(k_new.reshape(db, 1, N_KV_HEADS, HEAD_DIM))
        outs[6].append(v_new.reshape(db, 1, N_KV_HEADS, HEAD_DIM))
        outs[7].append(lf_new.reshape(db, 1, N_HEADS))
        outs[8].append(conv_new)

    stk = [jnp.stack(o) if o else None for o in outs]
    k_prompt, v_prompt = (a.reshape(depth, bp, seq, N_KV_HEADS, HEAD_DIM) for a in kv_all)
    return (yp, ys.reshape(db, 1, d), k_prompt, v_prompt, stk[2], stk[3], stk[4], stk[5], stk[6], stk[7],
            st_s.reshape(depth, db, SSM_HEADS, SSM_HEAD_DIM, D_STATE), jnp.transpose(stk[8], (0, 2, 1, 3)))
```

```python
import functools

import jax
import jax.numpy as jnp
from jax import lax
from jax.experimental import pallas as pl
from jax.experimental.pallas import tpu as pltpu

F32 = jnp.float32
BF16 = jnp.bfloat16
HI = lax.Precision.HIGHEST
EPS = 1e-6
NEG = -1e30

N_HEADS = 16
N_KV_HEADS = 8
HEAD_DIM = 64
ATT_WIDTH = N_HEADS * HEAD_DIM
KV_WIDTH = N_KV_HEADS * HEAD_DIM
SSM_HEADS = 32
SSM_HEAD_DIM = 64
SSM_GROUPS = 4
D_STATE = 128
CONV_WIDTH = 4
D_INNER = SSM_HEADS * SSM_HEAD_DIM
BC_WIDTH = SSM_GROUPS * D_STATE
CONV_DIM = D_INNER + 2 * BC_WIDTH
SSD_CHUNK = 128
N_EXPERTS = 16
N_EXPERT_GROUPS = 4
EXPERTS_PER_GROUP = N_EXPERTS // N_EXPERT_GROUPS

VMEM_LIMIT_BYTES = 56 * 1024 * 1024
LANES = 128

NT_DIMS = (((1,), (1,)), ((), ()))


def _params(*sem):
    return pltpu.CompilerParams(dimension_semantics=sem, vmem_limit_bytes=VMEM_LIMIT_BYTES)


def _sigmoid(x):
    return 1.0 / (1.0 + jnp.exp(-x))


def _silu(x):
    return x * _sigmoid(x)


def _softplus(x):
    return jnp.maximum(x, 0.0) + jnp.log1p(jnp.exp(-jnp.abs(x)))


def _dot(a, b, prec=None):
    return jnp.dot(a, b, precision=prec, preferred_element_type=F32)


def _dot_nt(a, b, prec=None):
    return lax.dot_general(a, b, NT_DIMS, precision=prec, preferred_element_type=F32)


def _eye(n):
    return (lax.broadcasted_iota(jnp.int32, (n, n), 0) == lax.broadcasted_iota(jnp.int32, (n, n), 1)).astype(F32)


def _head_expander(n_heads, width):
    r = lax.broadcasted_iota(jnp.int32, (n_heads, n_heads * width), 0)
    c = lax.broadcasted_iota(jnp.int32, (n_heads, n_heads * width), 1)
    return (c // width == r).astype(F32)


def _mod_kernel(c_ref, w_ref, b_ref, o_ref):
    o_ref[0] = _dot(_silu(c_ref[...]), w_ref[0], HI) + b_ref[0]


def _modulation(c_all, w_mod, b_mod, tn=512):
    depth, d, n = w_mod.shape
    rows = c_all.shape[0]
    return pl.pallas_call(
        _mod_kernel,
        grid=(depth, n // tn),
        in_specs=[pl.BlockSpec((rows, d), lambda l, j: (0, 0)),
                  pl.BlockSpec((1, d, tn), lambda l, j: (l, 0, j)),
                  pl.BlockSpec((1, 1, tn), lambda l, j: (l, 0, j))],
        out_specs=pl.BlockSpec((1, rows, tn), lambda l, j: (l, 0, j)),
        out_shape=jax.ShapeDtypeStruct((depth, rows, n), F32),
        compiler_params=_params("parallel", "parallel"),
        name="modulation",
    )(c_all, w_mod, b_mod.reshape(depth, 1, n))


def _norm_proj_kernel(x_ref, sc_ref, sh_ref, g_ref, w_ref, ws_ref, *refs, starts, prec):
    n_sec = len(starts) - 1
    out_refs, small_ref, h_ref = refs[:n_sec], refs[n_sec], refs[n_sec + 1]
    n = pl.program_id(2)

    @pl.when(n == 0)
    def _():
        x = x_ref[0]
        y = x * lax.rsqrt(jnp.mean(x * x, axis=-1, keepdims=True) + EPS)
        h = (y * g_ref[...]) * (1.0 + sc_ref[0]) + sh_ref[0]
        h = h.astype(h_ref.dtype)
        h_ref[...] = h
        small_ref[0] = _dot(h, ws_ref[...], prec)

    for i in range(n_sec):
        @pl.when(jnp.logical_and(n >= starts[i], n < starts[i + 1]))
        def _(i=i):
            out_refs[i][0] = _dot(h_ref[...], w_ref[...], prec).astype(out_refs[i].dtype)


def _norm_proj(x, sc, sh, g, w_main, w_small, sections, *, tm, tn, prec):
    b, l, d = x.shape
    mod_rows = sc.shape[1]
    mr = tm if mod_rows == l else 1
    starts = [0]
    for nt, _ in sections:
        starts.append(starts[-1] + nt)
    starts = tuple(starts)
    n_tiles = starts[-1]
    assert w_main.shape == (d, n_tiles * tn) and l % tm == 0
    ws_cols = w_small.shape[1]

    def mod_map(bi, mi, ni):
        return (bi, mi if mod_rows == l else 0, 0)

    def sec_map(i):
        lo, cnt = starts[i], sections[i][0]
        return lambda bi, mi, ni: (bi, mi, jnp.clip(ni - lo, 0, cnt - 1))

    out_specs = [pl.BlockSpec((1, tm, tn), sec_map(i)) for i in range(len(sections))]
    out_specs.append(pl.BlockSpec((1, tm, ws_cols), lambda bi, mi, ni: (bi, mi, 0)))
    out_shape = [jax.ShapeDtypeStruct((b, l, nt * tn), dt) for nt, dt in sections]
    out_shape.append(jax.ShapeDtypeStruct((b, l, ws_cols), F32))
    return pl.pallas_call(
        functools.partial(_norm_proj_kernel, starts=starts, prec=prec),
        grid=(b, l // tm, n_tiles),
        in_specs=[pl.BlockSpec((1, tm, d), lambda bi, mi, ni: (bi, mi, 0)),
                  pl.BlockSpec((1, mr, d), mod_map),
                  pl.BlockSpec((1, mr, d), mod_map),
                  pl.BlockSpec((1, d), lambda bi, mi, ni: (0, 0)),
                  pl.BlockSpec((d, tn), lambda bi, mi, ni: (0, ni)),
                  pl.BlockSpec((d, ws_cols), lambda bi, mi, ni: (0, 0))],
        out_specs=out_specs,
        out_shape=out_shape,
        scratch_shapes=[pltpu.VMEM((tm, d), w_main.dtype)],
        compiler_params=_params("parallel", "parallel", "arbitrary"),
        name="norm_proj",
    )(x, sc, sh, g.reshape(1, d), w_main, w_small)


def _head_norm(x, g):
    return x * lax.rsqrt(jnp.mean(x * x, axis=-1, keepdims=True) + EPS) * g


def _log_forget(small_ref, bf_ref):
    fl = small_ref[0][:, :N_HEADS] + bf_ref[...]
    return -_softplus(-fl)


def _qkv_post_prompt_kernel(qkv_ref, small_ref, gqk_ref, bf_ref, *refs):
    qh_ref, kh_ref, vh_ref, ko_ref, vo_ref, lf_ref = refs[-6:]
    ko_ref, vo_ref = ko_ref.at[0], vo_ref.at[0]
    n_qk = ATT_WIDTH + KV_WIDTH
    seg = (lax.broadcasted_iota(jnp.int32, (n_qk, LANES), 0) // HEAD_DIM
           == lax.broadcasted_iota(jnp.int32, (n_qk, LANES), 1)).astype(BF16)
    seg_t = (lax.broadcasted_iota(jnp.int32, (LANES, n_qk), 1) // HEAD_DIM
             == lax.broadcasted_iota(jnp.int32, (LANES, n_qk), 0)).astype(BF16)

    def two_term(a, m):
        hi = a.astype(BF16)
        return _dot(hi, m) + _dot((a - hi.astype(F32)).astype(BF16), m)

    x = qkv_ref[0, :, :n_qk]
    rs = lax.rsqrt(two_term(x * x, seg) * (1.0 / HEAD_DIM) + EPS)
    xn = x * two_term(rs, seg_t) * gqk_ref[...]
    for h in range(N_HEADS):
        qh_ref[0, h] = xn[:, h * HEAD_DIM:(h + 1) * HEAD_DIM].astype(qh_ref.dtype)
    ko_ref[0] = xn[:, ATT_WIDTH:]
    for g in range(N_KV_HEADS):
        lo = ATT_WIDTH + g * HEAD_DIM
        kh_ref[0, g] = xn[:, lo:lo + HEAD_DIM].astype(kh_ref.dtype)
        lo = ATT_WIDTH + KV_WIDTH + g * HEAD_DIM
        v = qkv_ref[0, :, lo:lo + HEAD_DIM].astype(vh_ref.dtype)
        rep = N_HEADS // N_KV_HEADS
        for r in range(rep):
            vh_ref[0, g, :, r * HEAD_DIM:(r + 1) * HEAD_DIM] = v
        vh_ref[0, g, :, rep * HEAD_DIM:] = jnp.ones((v.shape[0], rep * HEAD_DIM), vh_ref.dtype)
    vo_ref[0] = qkv_ref[0, :, ATT_WIDTH + KV_WIDTH:ATT_WIDTH + 2 * KV_WIDTH]
    lf_ref[0] = _log_forget(small_ref, bf_ref)


def _qkv_post_prompt(layer, depth, kv_all, qkv, small, g_q, g_k, b_f, *, tm):
    b, l, _ = qkv.shape
    tok = lambda w: pl.BlockSpec((1, tm, w), lambda bi, mi: (bi, mi, 0))
    head = lambda n, w=HEAD_DIM: pl.BlockSpec((1, n, tm, w), lambda bi, mi: (bi, 0, mi, 0))
    vec = lambda w: pl.BlockSpec((1, w), lambda bi, mi: (0, 0))
    kv_out = pl.BlockSpec((1, 1, tm, KV_WIDTH), lambda bi, mi: (layer, bi, mi, 0))
    v_width = 2 * (N_HEADS // N_KV_HEADS) * HEAD_DIM
    g_qk = jnp.concatenate([jnp.tile(g_q, N_HEADS) * HEAD_DIM ** -0.5, jnp.tile(g_k, N_KV_HEADS)]).reshape(1, -1)
    in_specs = [tok(ATT_WIDTH + 2 * KV_WIDTH), tok(LANES), vec(ATT_WIDTH + KV_WIDTH), vec(N_HEADS)]
    args = [qkv, small, g_qk, b_f.reshape(1, -1)]
    aliases = {}
    if kv_all is not None:
        in_specs += [pl.BlockSpec(memory_space=pl.ANY)] * 2
        aliases = {len(args): 3, len(args) + 1: 4}
        args += list(kv_all)
    return pl.pallas_call(
        _qkv_post_prompt_kernel,
        grid=(b, l // tm),
        in_specs=in_specs,
        out_specs=[head(N_HEADS), head(N_KV_HEADS), head(N_KV_HEADS, v_width), kv_out, kv_out, tok(N_HEADS)],
        out_shape=[jax.ShapeDtypeStruct((b, N_HEADS, l, HEAD_DIM), BF16),
                   jax.ShapeDtypeStruct((b, N_KV_HEADS, l, HEAD_DIM), BF16),
                   jax.ShapeDtypeStruct((b, N_KV_HEADS, l, v_width), BF16),
                   jax.ShapeDtypeStruct((depth, b, l, KV_WIDTH), F32),
                   jax.ShapeDtypeStruct((depth, b, l, KV_WIDTH), F32),
                   jax.ShapeDtypeStruct((b, l, N_HEADS), F32)],
        input_output_aliases=aliases,
        compiler_params=_params("parallel", "parallel"),
        name="qkv_post_prompt",
    )(*args)


def _qkv_post_sample_kernel(qkv_ref, small_ref, gq_ref, gk_ref, bf_ref, qo_ref, ko_ref, vo_ref, lf_ref):
    scale = HEAD_DIM ** -0.5
    for h in range(N_HEADS):
        sl = slice(h * HEAD_DIM, (h + 1) * HEAD_DIM)
        qo_ref[0, :, sl] = _head_norm(qkv_ref[0, :, sl], gq_ref[...]) * scale
    for g in range(N_KV_HEADS):
        lo = ATT_WIDTH + g * HEAD_DIM
        ko_ref[0, :, g * HEAD_DIM:(g + 1) * HEAD_DIM] = _head_norm(qkv_ref[0, :, lo:lo + HEAD_DIM], gk_ref[...])
    vo_ref[0] = qkv_ref[0, :, ATT_WIDTH + KV_WIDTH:ATT_WIDTH + 2 * KV_WIDTH]
    lf_ref[0] = _log_forget(small_ref, bf_ref)


def _qkv_post_sample(qkv, small, g_q, g_k, b_f):
    b, l, _ = qkv.shape
    tok = lambda w: pl.BlockSpec((1, l, w), lambda bi: (bi, 0, 0))
    vec = lambda w: pl.BlockSpec((1, w), lambda bi: (0, 0))
    return pl.pallas_call(
        _qkv_post_sample_kernel,
        grid=(b,),
        in_specs=[tok(ATT_WIDTH + 2 * KV_WIDTH), tok(LANES), vec(HEAD_DIM), vec(HEAD_DIM), vec(N_HEADS)],
        out_specs=[tok(ATT_WIDTH), tok(KV_WIDTH), tok(KV_WIDTH), tok(N_HEADS)],
        out_shape=[jax.ShapeDtypeStruct((b, l, ATT_WIDTH), F32),
                   jax.ShapeDtypeStruct((b, l, KV_WIDTH), F32),
                   jax.ShapeDtypeStruct((b, l, KV_WIDTH), F32),
                   jax.ShapeDtypeStruct((b, l, N_HEADS), F32)],
        compiler_params=_params("parallel"),
        name="qkv_post_sample",
    )(qkv, small, g_q.reshape(1, -1), g_k.reshape(1, -1), b_f.reshape(1, -1))


def _cumsum_kernel(lf_ref, cf_ref, cft_ref, *, chunk):
    l = lf_ref.shape[1]
    r = lax.broadcasted_iota(jnp.int32, (chunk, chunk), 0)
    c = lax.broadcasted_iota(jnp.int32, (chunk, chunk), 1)
    tril = (r >= c).astype(F32)
    eye = _eye(N_HEADS)
    carry = jnp.zeros((1, N_HEADS), F32)
    for i in range(l // chunk):
        sl = slice(i * chunk, (i + 1) * chunk)
        loc = _dot(tril, lf_ref[0, sl, :], HI) + carry
        cf_ref[0, sl, :] = loc
        cft_ref[0, :, sl] = _dot_nt(eye, loc, HI)
        carry = loc[chunk - 1:chunk, :]


def _cumsum_logf(lf, *, chunk=256):
    b, l, h = lf.shape
    return pl.pallas_call(
        functools.partial(_cumsum_kernel, chunk=chunk),
        grid=(b,),
        in_specs=[pl.BlockSpec((1, l, h), lambda bi: (bi, 0, 0))],
        out_specs=[pl.BlockSpec((1, l, h), lambda bi: (bi, 0, 0)), pl.BlockSpec((1, h, l), lambda bi: (bi, 0, 0))],
        out_shape=[jax.ShapeDtypeStruct((b, l, h), F32), jax.ShapeDtypeStruct((b, h, l), F32)],
        compiler_params=_params("parallel"),
        name="cumsum_logf",
    )(lf)


def _attn_prompt_kernel(q_ref, k_ref, v_ref, cf_ref, cft_ref, o_ref, m_ref, l_ref, acc_ref, *, tq):
    g, qi = pl.program_id(1), pl.program_id(2)
    rep = N_HEADS // N_KV_HEADS
    tk = tq
    width = rep * HEAD_DIM
    lane16 = lax.broadcasted_iota(jnp.int32, (tq, N_HEADS), 1)
    cfb = cf_ref[0]
    cq = [jnp.broadcast_to(jnp.sum(jnp.where(lane16 == rep * g + r, cfb, 0.0), axis=-1, keepdims=True), (tq, width))
          for r in range(rep)]
    head_of_lane = lax.broadcasted_iota(jnp.int32, (tq, width), 1) // HEAD_DIM

    def per_head(vals):
        out = vals[0]
        for r in range(1, rep):
            out = jnp.where(head_of_lane == r, vals[r], out)
        return out

    m_ref[...] = jnp.full(m_ref.shape, NEG, F32)
    l_ref[...] = jnp.zeros(l_ref.shape, F32)
    acc_ref[...] = jnp.zeros(acc_ref.shape, F32)

    def block(ki, diagonal):
        keys = pl.ds(pl.multiple_of(ki * tk, tk), tk)
        alphas, pvs = [], []
        for r in range(rep):
            t = _dot_nt(q_ref[0, r], k_ref[0, 0, keys, :]) - cft_ref[0, 0, ki, r:r + 1, :]
            if diagonal:
                row = lax.broadcasted_iota(jnp.int32, (tq, tk), 0)
                col = lax.broadcasted_iota(jnp.int32, (tq, tk), 1)
                t = jnp.where(col <= row, t, NEG)
            m_prev = m_ref[r]
            m_new = jnp.maximum(m_prev, jnp.max(t, axis=-1, keepdims=True) + cq[r])
            alpha = jnp.exp(m_prev - m_new)
            p = jnp.exp(t + jnp.tile(cq[r] - m_new, (1, tk // width)))
            pv = _dot(p.astype(v_ref.dtype), v_ref[0, 0, keys, :])
            l_ref[r] = alpha * l_ref[r] + pv[:, width:]
            m_ref[r] = m_new
            alphas.append(alpha)
            pvs.append(pv[:, :width])
        acc_ref[...] = per_head(alphas) * acc_ref[...] + per_head(pvs)

    def body(ki, carry):
        block(ki, False)
        return carry

    lax.fori_loop(0, qi, body, 0)
    block(qi, True)
    o_ref[0] = (acc_ref[...] / per_head([l_ref[r] for r in range(rep)])).astype(o_ref.dtype)


def _attention_prompt(qh, kh, vh_ext, cf, cft, *, tq):
    b, _, l, _ = qh.shape
    rep = N_HEADS // N_KV_HEADS
    width = rep * HEAD_DIM
    nq = l // tq
    cft_tiles = cft.reshape(b, N_KV_HEADS, rep, nq, tq).transpose(0, 1, 3, 2, 4)
    return pl.pallas_call(
        functools.partial(_attn_prompt_kernel, tq=tq),
        grid=(b, N_KV_HEADS, nq),
        in_specs=[pl.BlockSpec((1, rep, tq, HEAD_DIM), lambda bi, g, qi: (bi, g, qi, 0)),
                  pl.BlockSpec((1, 1, l, HEAD_DIM), lambda bi, g, qi: (bi, g, 0, 0)),
                  pl.BlockSpec((1, 1, l, 2 * width), lambda bi, g, qi: (bi, g, 0, 0)),
                  pl.BlockSpec((1, tq, N_HEADS), lambda bi, g, qi: (bi, qi, 0)),
                  pl.BlockSpec((1, 1, nq, rep, tq), lambda bi, g, qi: (bi, g, 0, 0, 0))],
        out_specs=pl.BlockSpec((1, tq, width), lambda bi, g, qi: (bi, qi, g)),
        out_shape=jax.ShapeDtypeStruct((b, l, ATT_WIDTH), BF16),
        scratch_shapes=[pltpu.VMEM((rep, tq, width), F32), pltpu.VMEM((rep, tq, width), F32),
                        pltpu.VMEM((tq, width), F32)],
        compiler_params=_params("parallel", "parallel", "arbitrary"),
        name="attention_prompt",
    )(qh, kh, vh_ext, cf, cft_tiles)


def _attn_decode_kernel(pt_ref, qt_ref, q_ref, kn_ref, vnt_ref, lfn_ref, *refs, n_pages, page):
    del pt_ref
    k_refs, v_refs, lf_refs = refs[:n_pages], refs[n_pages:2 * n_pages], refs[2 * n_pages:3 * n_pages]
    ot_ref, qb_ref, s_ref = refs[3 * n_pages:]
    rep = N_HEADS // N_KV_HEADS

    hrow = lax.broadcasted_iota(jnp.int32, (N_HEADS, N_HEADS * page), 0)
    hcol = lax.broadcasted_iota(jnp.int32, (N_HEADS, N_HEADS * page), 1) // page
    q_all = _dot(qt_ref[0], (hrow == hcol).astype(F32), HI)
    for h in range(N_HEADS):
        qb_ref[h] = q_all[:, h * page:(h + 1) * page]

    j = lax.broadcasted_iota(jnp.int32, (page, page), 0)
    t = lax.broadcasted_iota(jnp.int32, (page, page), 1)
    later = (j > t).astype(F32)
    totals = [jnp.sum(lf_refs[p][0, 0], axis=-1, keepdims=True) for p in range(n_pages)]
    carry = lfn_ref[0]
    for p in reversed(range(n_pages)):
        s_ref[:, p * page:(p + 1) * page] = _dot(lf_refs[p][0, 0], later, HI) + carry
        carry = carry + totals[p]

    for p in range(n_pages):
        cols = slice(p * page, (p + 1) * page)
        for g in range(N_KV_HEADS):
            kt = k_refs[p][0, 0, g]
            for r in range(rep):
                h = g * rep + r
                s_ref[h:h + 1, cols] = s_ref[h:h + 1, cols] + jnp.sum(kt * qb_ref[h], axis=0, keepdims=True)

    s = s_ref[...]
    s_new = jnp.sum(q_ref[0] * kn_ref[0], axis=-1, keepdims=True)
    m = jnp.maximum(jnp.max(s, axis=-1, keepdims=True), s_new)
    pmat = jnp.exp(s - m)
    e_new = jnp.exp(s_new - m)
    denom = jnp.sum(pmat, axis=-1, keepdims=True) + e_new
    s_ref[...] = pmat

    lane = lax.broadcasted_iota(jnp.int32, (HEAD_DIM, N_HEADS), 1)
    ot = jnp.zeros((HEAD_DIM, N_HEADS), F32)
    for g in range(N_KV_HEADS):
        acc = [jnp.zeros((HEAD_DIM, page), F32) for _ in range(rep)]
        for p in range(n_pages):
            vt = v_refs[p][0, 0, g]
            for r in range(rep):
                h = g * rep + r
                acc[r] = acc[r] + vt * s_ref[h:h + 1, p * page:(p + 1) * page]
        for r in range(rep):
            ot = jnp.where(lane == g * rep + r, jnp.sum(acc[r], axis=-1, keepdims=True), ot)
    eye = _eye(N_HEADS)
    e_row = jnp.sum(eye * e_new, axis=0, keepdims=True)
    d_row = jnp.sum(eye * denom, axis=0, keepdims=True)
    ot_ref[0] = (ot + vnt_ref[0] * e_row) / d_row


def _attention_decode(layer, page_table, q, k_new, v_new, lf_new, cache_kt, cache_vt, cache_lft):
    db, n_pages = page_table.shape
    page = cache_kt.shape[-1]
    rep = N_HEADS // N_KV_HEADS
    qt = jnp.swapaxes(q, 1, 2)
    kn = jnp.repeat(k_new, rep, axis=1)
    vnt = jnp.swapaxes(jnp.repeat(v_new, rep, axis=1), 1, 2)
    rows = lambda: pl.BlockSpec((1, N_HEADS, HEAD_DIM), lambda bi, pt: (bi, 0, 0))
    cols = lambda: pl.BlockSpec((1, HEAD_DIM, N_HEADS), lambda bi, pt: (bi, 0, 0))
    kv_spec = lambda p: pl.BlockSpec((1, 1, N_KV_HEADS, HEAD_DIM, page),
                                     lambda bi, pt: (layer, pt[bi, p], 0, 0, 0))
    lf_spec = lambda p: pl.BlockSpec((1, 1, N_HEADS, page), lambda bi, pt: (layer, pt[bi, p], 0, 0))
    pages = range(n_pages)
    grid_spec = pltpu.PrefetchScalarGridSpec(
        num_scalar_prefetch=1,
        grid=(db,),
        in_specs=[cols(), rows(), rows(), cols(), pl.BlockSpec((1, N_HEADS, 1), lambda bi, pt: (bi, 0, 0))]
        + [kv_spec(p) for p in pages] + [kv_spec(p) for p in pages] + [lf_spec(p) for p in pages],
        out_specs=cols(),
        scratch_shapes=[pltpu.VMEM((N_HEADS, HEAD_DIM, page), F32), pltpu.VMEM((N_HEADS, n_pages * page), F32)])
    return pl.pallas_call(
        functools.partial(_attn_decode_kernel, n_pages=n_pages, page=page),
        grid_spec=grid_spec,
        out_shape=jax.ShapeDtypeStruct((db, HEAD_DIM, N_HEADS), F32),
        compiler_params=_params("parallel"),
        name="attention_decode",
    )(page_table, qt, q, kn, vnt, lf_new[:, :, None],
      *([cache_kt] * n_pages), *([cache_vt] * n_pages), *([cache_lft] * n_pages))


def _ssd_prompt_kernel(xbc_ref, z_ref, dt_ref, cw_ref, cb_ref, dtb_ref, alog_ref, dskip_ref, gssm_ref,
                       y_ref, st_ref, cv_ref, ext_ref, state_ref, yd_ref):
    c = pl.program_id(1)
    t = SSD_CHUNK
    hist = 8
    hpg = SSM_HEADS // SSM_GROUPS
    gw = hpg * SSM_HEAD_DIM

    @pl.when(c == 0)
    def _():
        ext_ref[0:hist, :] = jnp.zeros((hist, CONV_DIM), F32)
        state_ref[...] = jnp.zeros(state_ref.shape, F32)

    ext_ref[hist:hist + t, :] = xbc_ref[0].astype(F32)
    conv = cb_ref[...] + ext_ref[hist:hist + t, :] * cw_ref[CONV_WIDTH - 1:CONV_WIDTH, :]
    for j in range(1, CONV_WIDTH):
        conv = conv + ext_ref[hist - j:hist - j + t, :] * cw_ref[CONV_WIDTH - 1 - j:CONV_WIDTH - j, :]
    tail = ext_ref[t:t + hist, :]
    cv_ref[0] = tail[hist - (CONV_WIDTH - 1):, :]
    ext_ref[0:hist, :] = tail
    u = _silu(conv)
    xs = u[:, :D_INNER]

    dt = _softplus(dt_ref[0][:, :SSM_HEADS] + dtb_ref[...])
    a = -jnp.exp(alog_ref[...])
    r = lax.broadcasted_iota(jnp.int32, (t, t), 0)
    cidx = lax.broadcasted_iota(jnp.int32, (t, t), 1)
    tri = r >= cidx
    acs = _dot(tri.astype(F32), dt * a, HI)
    acs_t = _dot_nt(_eye(SSM_HEADS), acs, HI)
    expand = _head_expander(SSM_HEADS, SSM_HEAD_DIM).astype(BF16)

    def per_channel(x):
        hi = x.astype(BF16)
        return _dot(hi, expand) + _dot((x - hi.astype(F32)).astype(BF16), expand)

    dt_x = per_channel(dt)
    ea_x = per_channel(jnp.exp(acs))
    dec_x = per_channel(jnp.exp(acs[t - 1:t, :] - acs))
    xd = xs * dt_x
    xdd = (xd * dec_x).astype(BF16)
    xd_b = xd.astype(BF16)
    cdec_x = ea_x[t - 1:t, :]

    for g in range(SSM_GROUPS):
        bg = u[:, D_INNER + g * D_STATE:D_INNER + (g + 1) * D_STATE]
        cg = u[:, D_INNER + BC_WIDTH + g * D_STATE:D_INNER + BC_WIDTH + (g + 1) * D_STATE]
        bg_b, cg_b = bg.astype(BF16), cg.astype(BF16)
        cb = _dot_nt(cg_b, bg_b)
        gs = slice(g * gw, (g + 1) * gw)
        s_prev = state_ref[g]
        y_off = _dot(cg_b, s_prev.astype(BF16)) * ea_x[:, gs]
        state_ref[g] = s_prev * cdec_x[:, gs] + _dot(bg.T.astype(BF16), xdd[:, gs])
        for hh in range(hpg):
            h = g * hpg + hh
            seg = acs[:, h:h + 1] - acs_t[h:h + 1, :]
            lmat = jnp.exp(jnp.where(tri, seg, NEG))
            hs = slice(h * SSM_HEAD_DIM, (h + 1) * SSM_HEAD_DIM)
            yd_ref[:, hs] = _dot((cb * lmat).astype(BF16), xd_b[:, hs])
        yd_ref[:, gs] = yd_ref[:, gs] + y_off

    y = (yd_ref[...] + dskip_ref[...] * xs) * _silu(z_ref[0].astype(F32))
    for g in range(SSM_GROUPS):
        gs = slice(g * gw, (g + 1) * gw)
        yg = y[:, gs]
        yg = yg * lax.rsqrt(jnp.mean(yg * yg, axis=-1, keepdims=True) + EPS)
        y_ref[0, :, gs] = (yg * gssm_ref[:, gs]).astype(y_ref.dtype)

    @pl.when(c == pl.num_programs(1) - 1)
    def _():
        for g in range(SSM_GROUPS):
            st_ref[0, g * gw:(g + 1) * gw, :] = state_ref[g].T


def _ssd_prompt(xbc, z, dtr, conv_w, conv_b, dt_bias, a_log, d_skip, g_ssm):
    b, l, _ = xbc.shape
    t = SSD_CHUNK
    gw = D_INNER // SSM_GROUPS
    vec = lambda w: pl.BlockSpec((1, w), lambda bi, ci: (0, 0))
    y, st, cv = pl.pallas_call(
        _ssd_prompt_kernel,
        grid=(b, l // t),
        in_specs=[pl.BlockSpec((1, t, CONV_DIM), lambda bi, ci: (bi, ci, 0)),
                  pl.BlockSpec((1, t, D_INNER), lambda bi, ci: (bi, ci, 0)),
                  pl.BlockSpec((1, t, LANES), lambda bi, ci: (bi, ci, 1)),
                  pl.BlockSpec((CONV_WIDTH, CONV_DIM), lambda bi, ci: (0, 0)),
                  vec(CONV_DIM), vec(SSM_HEADS), vec(SSM_HEADS), vec(D_INNER), vec(D_INNER)],
        out_specs=[pl.BlockSpec((1, t, D_INNER), lambda bi, ci: (bi, ci, 0)),
                   pl.BlockSpec((1, D_INNER, D_STATE), lambda bi, ci: (bi, 0, 0)),
                   pl.BlockSpec((1, CONV_WIDTH - 1, CONV_DIM), lambda bi, ci: (bi, 0, 0))],
        out_shape=[jax.ShapeDtypeStruct((b, l, D_INNER), BF16),
                   jax.ShapeDtypeStruct((b, D_INNER, D_STATE), F32),
                   jax.ShapeDtypeStruct((b, CONV_WIDTH - 1, CONV_DIM), F32)],
        scratch_shapes=[pltpu.VMEM((t + 8, CONV_DIM), F32),
                        pltpu.VMEM((SSM_GROUPS, D_STATE, gw), F32),
                        pltpu.VMEM((t, D_INNER), F32)],
        compiler_params=_params("parallel", "arbitrary"),
        name="ssd_prompt",
    )(xbc, z, dtr, conv_w, conv_b.reshape(1, -1), dt_bias.reshape(1, -1), a_log.reshape(1, -1),
      jnp.repeat(d_skip, SSM_HEAD_DIM).reshape(1, -1), g_ssm.reshape(1, -1))
    return y, st.reshape(b, SSM_HEADS, SSM_HEAD_DIM, D_STATE), cv


def _ssm_step_prep_kernel(xbc_ref, conv0_ref, dt_ref, cw_ref, cb_ref, dtb_ref, alog_ref, dskip_ref,
                          convn_ref, xd3_ref, da_ref, b_ref, c_ref, dx_ref):
    w = CONV_WIDTH
    db = xbc_ref.shape[0]
    xbc = xbc_ref[...]
    conv = cb_ref[...] + xbc * cw_ref[w - 1:w, :]
    for i in range(w - 1):
        conv = conv + conv0_ref[0, i] * cw_ref[i:i + 1, :]
    for i in range(w - 2):
        convn_ref[i] = conv0_ref[0, i + 1]
    convn_ref[w - 2] = xbc
    u = _silu(conv)
    xs = u[:, :D_INNER]
    b_ref[...] = u[:, D_INNER:D_INNER + BC_WIDTH]
    c_ref[...] = u[:, D_INNER + BC_WIDTH:]
    dt = _softplus(dt_ref[...][:, :SSM_HEADS] + dtb_ref[...])
    da_ref[...] = jnp.exp(dt * -jnp.exp(alog_ref[...]))
    xd = xs * _dot(dt, _head_expander(SSM_HEADS, SSM_HEAD_DIM), HI)
    dx_ref[...] = dskip_ref[...] * xs
    for j in range(D_INNER // LANES):
        sl = slice(j * LANES, (j + 1) * LANES)
        xt = xd[:, sl].T
        hi = xt.astype(BF16)
        rest = xt - hi.astype(F32)
        mid = rest.astype(BF16)
        xd3_ref[sl, 0:db] = hi
        xd3_ref[sl, db:2 * db] = mid
        xd3_ref[sl, 2 * db:3 * db] = (rest - mid.astype(F32)).astype(BF16)


def _ssm_step_prep(layer, xbc, conv_state, dtr, conv_w, conv_b, dt_bias, a_log, d_skip):
    db = xbc.shape[0]
    assert db == LANES
    full = lambda *s: pl.BlockSpec(s, lambda i: tuple(0 for _ in s))
    hist = CONV_WIDTH - 1
    return pl.pallas_call(
        _ssm_step_prep_kernel,
        grid=(1,),
        in_specs=[full(db, CONV_DIM), pl.BlockSpec((1, hist, db, CONV_DIM), lambda i: (layer, 0, 0, 0)),
                  pl.BlockSpec((db, LANES), lambda i: (0, 1)),
                  full(CONV_WIDTH, CONV_DIM), full(1, CONV_DIM), full(1, SSM_HEADS), full(1, SSM_HEADS),
                  full(1, D_INNER)],
        out_specs=[full(hist, db, CONV_DIM), full(D_INNER, 3 * db), full(db, SSM_HEADS), full(db, BC_WIDTH),
                   full(db, BC_WIDTH), full(db, D_INNER)],
        out_shape=[jax.ShapeDtypeStruct((hist, db, CONV_DIM), F32),
                   jax.ShapeDtypeStruct((D_INNER, 3 * db), BF16),
                   jax.ShapeDtypeStruct((db, SSM_HEADS), F32),
                   jax.ShapeDtypeStruct((db, BC_WIDTH), F32),
                   jax.ShapeDtypeStruct((db, BC_WIDTH), F32),
                   jax.ShapeDtypeStruct((db, D_INNER), F32)],
        compiler_params=_params("arbitrary"),
        name="ssm_step_prep",
    )(xbc, conv_state, dtr, conv_w, conv_b.reshape(1, -1), dt_bias.reshape(1, -1), a_log.reshape(1, -1),
      jnp.repeat(d_skip, SSM_HEAD_DIM).reshape(1, -1))


def _ssm_step_state_kernel(da_ref, s_ref, xd3_ref, b_ref, c_ref, *refs):
    so_ref, yt_ref = refs[-2:]
    s_ref, so_ref = s_ref.at[0], so_ref.at[0]
    bi = pl.program_id(0)
    db = xd3_ref.shape[1] // 3
    hpg = SSM_HEADS // SSM_GROUPS

    @pl.when(bi == 0)
    def _():
        yt_ref[...] = jnp.zeros(yt_ref.shape, F32)

    pick = lax.broadcasted_iota(jnp.int32, (3 * db, D_STATE), 0) % db == bi
    xcol = _dot(xd3_ref[...], jnp.where(pick, 1.0, 0.0).astype(BF16))
    lane = lax.broadcasted_iota(jnp.int32, (SSM_HEAD_DIM, db), 1)
    for h in range(SSM_HEADS):
        rows = slice(h * SSM_HEAD_DIM, (h + 1) * SSM_HEAD_DIM)
        ns = slice((h // hpg) * D_STATE, (h // hpg + 1) * D_STATE)
        s_new = s_ref[0, rows, :] * da_ref[bi, h] + xcol[rows, :] * b_ref[0, :, ns]
        so_ref[0, rows, :] = s_new
        y_col = jnp.sum(s_new * c_ref[0, :, ns], axis=-1, keepdims=True)
        yt_ref[rows, :] = jnp.where(lane == bi, y_col, yt_ref[rows, :])


def _ssm_step_state(layer, state_all, new_state_all, xd3, da, bm, cm):
    depth, db, rows, _ = state_all.shape
    seq3 = lambda w: pl.BlockSpec((1, 1, w), lambda bi: (bi, 0, 0))
    st_spec = pl.BlockSpec((1, 1, rows, D_STATE), lambda bi: (layer, bi, 0, 0))
    in_specs = [pl.BlockSpec(memory_space=pltpu.SMEM), st_spec, pl.BlockSpec((rows, 3 * db), lambda bi: (0, 0)),
                seq3(BC_WIDTH), seq3(BC_WIDTH)]
    args = [da, state_all, xd3, bm[:, None, :], cm[:, None, :]]
    aliases = {}
    if new_state_all is not None:
        in_specs.append(pl.BlockSpec(memory_space=pl.ANY))
        args.append(new_state_all)
        aliases = {len(args) - 1: 0}
    return pl.pallas_call(
        _ssm_step_state_kernel,
        grid=(db,),
        in_specs=in_specs,
        out_specs=[st_spec, pl.BlockSpec((rows, db), lambda bi: (0, 0))],
        out_shape=[jax.ShapeDtypeStruct(state_all.shape, F32), jax.ShapeDtypeStruct((rows, db), F32)],
        input_output_aliases=aliases,
        compiler_params=_params("arbitrary"),
        name="ssm_step_state",
    )(*args)


def _ssm_step_gate_kernel(yt_ref, dx_ref, z_ref, gssm_ref, o_ref):
    gw = D_INNER // SSM_GROUPS
    for g in range(SSM_GROUPS):
        gs = slice(g * gw, (g + 1) * gw)
        y = jnp.concatenate([yt_ref[g * gw + j * LANES:g * gw + (j + 1) * LANES, :].T for j in range(gw // LANES)],
                            axis=-1)
        yg = (y + dx_ref[:, gs]) * _silu(z_ref[:, gs])
        o_ref[:, gs] = yg * lax.rsqrt(jnp.mean(yg * yg, axis=-1, keepdims=True) + EPS) * gssm_ref[:, gs]


def _ssm_step_gate(yt, dx, z, g_ssm):
    db = dx.shape[0]
    assert db == LANES
    full = lambda *s: pl.BlockSpec(s, lambda i: tuple(0 for _ in s))
    return pl.pallas_call(
        _ssm_step_gate_kernel,
        grid=(1,),
        in_specs=[full(D_INNER, db), full(db, D_INNER), full(db, D_INNER), full(1, D_INNER)],
        out_specs=full(db, D_INNER),
        out_shape=jax.ShapeDtypeStruct((db, D_INNER), F32),
        compiler_params=_params("arbitrary"),
        name="ssm_step_gate",
    )(yt, dx, z, g_ssm.reshape(1, -1))


def _route(logits):
    e = jnp.exp(logits - jnp.max(logits, axis=-1, keepdims=True))
    probs = e / jnp.sum(e, axis=-1, keepdims=True)
    lane_i = lax.broadcasted_iota(jnp.int32, probs.shape, 1)
    lane = lane_i.astype(F32)
    grp = (lane_i // EXPERTS_PER_GROUP).astype(F32)
    best = jnp.max(jnp.where(grp == 0.0, probs, -1.0), axis=-1, keepdims=True)
    sel = jnp.zeros(best.shape, F32)
    for g in range(1, N_EXPERT_GROUPS):
        gmax = jnp.max(jnp.where(grp == float(g), probs, -1.0), axis=-1, keepdims=True)
        sel = jnp.where(gmax > best, float(g), sel)
        best = jnp.maximum(best, gmax)
    pin = jnp.where(grp == sel, probs, -1.0)
    v1 = jnp.max(pin, axis=-1, keepdims=True)
    i1 = jnp.min(jnp.where(pin == v1, lane, float(N_EXPERTS)), axis=-1, keepdims=True)
    pin2 = jnp.where(lane == i1, -1.0, pin)
    v2 = jnp.max(pin2, axis=-1, keepdims=True)
    i2 = jnp.min(jnp.where(pin2 == v2, lane, float(N_EXPERTS)), axis=-1, keepdims=True)
    tot = v1 + v2
    return jnp.where(lane == i1, v1 / tot, 0.0) + jnp.where(lane == i2, v2 / tot, 0.0)


MERGE_ROW_PARTS = 2
MERGE_MIN_PART_ROWS = 256


def _merge_kernel(oa_ref, ys_ref, ga_ref, gb_ref, x_ref, gt_ref, sc_ref, sh_ref, g2_ref,
                  wpa_ref, wpb_ref, wo_ref, wr_ref, br_ref, x1_ref, h2_ref, comb_ref, *, prec):
    cdt = wpa_ref.dtype
    tm = x_ref.shape[1]
    n_parts = MERGE_ROW_PARTS if tm >= MERGE_ROW_PARTS * MERGE_MIN_PART_ROWS else 1
    part = tm // n_parts
    for i in range(n_parts):
        rows = slice(i * part, (i + 1) * part)
        mrows = rows if gt_ref.shape[1] == tm else slice(None)
        pa = _dot(oa_ref[0, rows, :].astype(cdt), wpa_ref[0], prec)
        pb = _dot(ys_ref[0, rows, :].astype(cdt), wpb_ref[0], prec)
        m = _sigmoid(ga_ref[0, rows, :].astype(F32)) * pa + _sigmoid(gb_ref[0, rows, :].astype(F32)) * pb
        x1 = x_ref[0, rows, :] + gt_ref[0, mrows, :] * _dot(m.astype(cdt), wo_ref[0], prec)
        x1_ref[0, rows, :] = x1
        y = x1 * lax.rsqrt(jnp.mean(x1 * x1, axis=-1, keepdims=True) + EPS)
        h2 = (y * g2_ref[...]) * (1.0 + sc_ref[0, mrows, :]) + sh_ref[0, mrows, :]
        h2_ref[0, rows, :] = h2.astype(h2_ref.dtype)
        comb_ref[0, rows, :] = _route(_dot(h2, wr_ref[...], HI) + br_ref[...])


def _merge(layer, o_att, y_ssm, gates, x, gt1, sc2, sh2, g2, w_pa, w_pb, w_o, w_router, b_router, *,
           tm, h2_dtype, prec):
    b, l, d = x.shape
    mod_rows = gt1.shape[1]
    mr = tm if mod_rows == l else 1
    tok = lambda w, j=0: pl.BlockSpec((1, tm, w), lambda bi, mi: (bi, mi, j))
    mod = pl.BlockSpec((1, mr, d), lambda bi, mi: (bi, mi if mod_rows == l else 0, 0))
    full = lambda a: pl.BlockSpec(a.shape, lambda bi, mi: tuple(0 for _ in a.shape))
    per_layer = lambda a: pl.BlockSpec((1,) + a.shape[1:], lambda bi, mi: (layer, 0, 0))
    g2r, brr = g2.reshape(1, d), b_router.reshape(1, -1)
    return pl.pallas_call(
        functools.partial(_merge_kernel, prec=prec),
        grid=(b, l // tm),
        in_specs=[tok(ATT_WIDTH), tok(D_INNER), tok(d, 0), tok(d, 1), tok(d), mod, mod, mod, full(g2r),
                  per_layer(w_pa), per_layer(w_pb), per_layer(w_o), full(w_router), full(brr)],
        out_specs=[tok(d), tok(d), tok(N_EXPERTS)],
        out_shape=[jax.ShapeDtypeStruct((b, l, d), F32),
                   jax.ShapeDtypeStruct((b, l, d), h2_dtype),
                   jax.ShapeDtypeStruct((b, l, N_EXPERTS), F32)],
        compiler_params=_params("parallel", "parallel"),
        name="merge",
    )(o_att, y_ssm, gates, gates, x, gt1, sc2, sh2, g2r, w_pa, w_pb, w_o, w_router, brr)


def _moe_kernel(h_ref, comb_ref, x_ref, gt_ref, wg_ref, wu_ref, wd_ref, o_ref, acc_ref, *, prec):
    e = pl.program_id(2)

    @pl.when(e == 0)
    def _():
        acc_ref[...] = jnp.zeros(acc_ref.shape, F32)

    h = h_ref[0]
    comb = comb_ref[0]
    lane = lax.broadcasted_iota(jnp.int32, comb.shape, 1)
    per_step = wg_ref.shape[1]
    acts = []
    for j in range(per_step):
        w_e = jnp.sum(jnp.where(lane == e * per_step + j, comb, 0.0), axis=-1, keepdims=True)
        act = _silu(_dot(h, wg_ref[0, j], prec)) * _dot(h, wu_ref[0, j], prec) * w_e
        acts.append(act.astype(wd_ref.dtype))
    dff, d = wd_ref.shape[2], wd_ref.shape[3]
    acc_ref[...] += _dot(jnp.concatenate(acts, axis=-1), wd_ref[0].reshape(per_step * dff, d), prec)

    @pl.when(e == pl.num_programs(2) - 1)
    def _():
        o_ref[0] = x_ref[0] + gt_ref[0] * acc_ref[...]


def _moe(layer, h2, comb, x1, gt2, w_gate, w_up, w_down, *, tm, prec):
    b, l, d = x1.shape
    _, ne, _, dff = w_gate.shape
    mod_rows = gt2.shape[1]
    mr = tm if mod_rows == l else 1
    tok = lambda w: pl.BlockSpec((1, tm, w), lambda bi, mi, e: (bi, mi, 0))
    eps = MOE_EXPERTS_PER_STEP
    return pl.pallas_call(
        functools.partial(_moe_kernel, prec=prec),
        grid=(b, l // tm, ne // eps),
        in_specs=[tok(d), tok(N_EXPERTS), tok(d),
                  pl.BlockSpec((1, mr, d), lambda bi, mi, e: (bi, mi if mod_rows == l else 0, 0)),
                  pl.BlockSpec((1, eps, d, dff), lambda bi, mi, e: (layer, e, 0, 0)),
                  pl.BlockSpec((1, eps, d, dff), lambda bi, mi, e: (layer, e, 0, 0)),
                  pl.BlockSpec((1, eps, dff, d), lambda bi, mi, e: (layer, e, 0, 0))],
        out_specs=tok(d),
        out_shape=jax.ShapeDtypeStruct((b, l, d), F32),
        scratch_shapes=[pltpu.VMEM((tm, d), F32)],
        compiler_params=_params("parallel", "parallel", "arbitrary"),
        name="moe",
    )(h2, comb, x1, gt2, w_gate, w_up, w_down)


TM_PROJ = 2048
TM_POST = 256
TQ_ATTN = 512
TM_MERGE = 512
TM_MOE = 1024
MOE_EXPERTS_PER_STEP = 2
IN_TILE = 512


def _split_w_in(w):
    o_q, o_k, o_v = 0, ATT_WIDTH, ATT_WIDTH + KV_WIDTH
    o_f = o_v + KV_WIDTH
    o_z = o_f + N_HEADS
    o_x = o_z + D_INNER
    o_dt = o_x + CONV_DIM
    o_ga = o_dt + SSM_HEADS
    d_model = w.shape[0]
    main = jnp.concatenate([w[:, o_q:o_f], w[:, o_z:o_x], w[:, o_x:o_dt], w[:, o_ga:]], axis=1)
    small = jnp.zeros((d_model, 2 * LANES), w.dtype)
    small = small.at[:, :N_HEADS].set(w[:, o_f:o_z]).at[:, LANES:LANES + SSM_HEADS].set(w[:, o_dt:o_ga])
    return main, small


def _sections(rest_dtype):
    qkv = (ATT_WIDTH + 2 * KV_WIDTH) // IN_TILE
    return ((qkv, F32), (D_INNER // IN_TILE, rest_dtype), (CONV_DIM // IN_TILE, rest_dtype),
            (2 * ATT_WIDTH // IN_TILE, rest_dtype))


def kernel(x_prompt, x_sample, cache_k, cache_v, cache_logf, state_ssm, state_conv, page_table, c_prompt, c_sample,
           w_mod, b_mod, g_norm1, g_norm2, w_in, b_f, g_q, g_k, conv_w, conv_b, dt_bias, a_log, d_skip, g_ssm,
           w_pa, w_pb, w_o, w_router, b_router, w_gate, w_up, w_down):
    depth = w_in.shape[0]
    bp, seq, d = x_prompt.shape
    db = x_sample.shape[0]
    assert x_sample.shape[1] == 1

    mod = _modulation(jnp.concatenate([c_prompt, c_sample], axis=0), w_mod, b_mod)
    yp = x_prompt
    ys = x_sample.reshape(1, db, d)
    outs = [[] for _ in range(9)]
    st_s = kv_all = None

    cache_kt = jnp.transpose(cache_k, (0, 1, 3, 4, 2))
    cache_vt = jnp.transpose(cache_v, (0, 1, 3, 4, 2))
    cache_lft = jnp.transpose(cache_logf, (0, 1, 3, 2))
    conv_state = jnp.transpose(state_conv, (0, 2, 1, 3))
    ssm_state = state_ssm.reshape(depth, db, D_INNER, D_STATE)
    bf = lambda w: w.astype(BF16)
    w_pa_b, w_pb_b, w_o_b, w_gate_b, w_up_b, w_down_b = bf(w_pa), bf(w_pb), bf(w_o), bf(w_gate), bf(w_up), bf(w_down)

    for l in range(depth):
        mp = [mod[l, :bp, i * d:(i + 1) * d][:, None, :] for i in range(6)]
        ms = [mod[l, bp:, i * d:(i + 1) * d][None] for i in range(6)]
        w_main, w_small = _split_w_in(w_in[l])

        qkv, z, xbc, gates, small = _norm_proj(
            yp, mp[1], mp[0], g_norm1[l], w_main.astype(BF16), w_small.astype(BF16), _sections(BF16),
            tm=min(TM_PROJ, seq), tn=IN_TILE, prec=None)
        qh, kh, vh, k_all, v_all, lf = _qkv_post_prompt(l, depth, kv_all, qkv, small, g_q[l], g_k[l], b_f[l],
                                                        tm=min(TM_POST, seq))
        kv_all = (k_all, v_all)
        cf, cft = _cumsum_logf(lf)
        o_att = _attention_prompt(qh, kh, vh, cf, cft, tq=min(TQ_ATTN, seq))
        y_ssm, st_p, cv_p = _ssd_prompt(xbc, z, small, conv_w[l], conv_b[l], dt_bias[l], a_log[l], d_skip[l],
                                        g_ssm[l])
        x1, h2, comb = _merge(l, o_att, y_ssm, gates, yp, mp[2], mp[4], mp[3], g_norm2[l],
                              w_pa_b, w_pb_b, w_o_b, w_router, b_router,
                              tm=min(TM_MERGE, seq), h2_dtype=BF16, prec=None)
        yp = _moe(l, h2, comb, x1, mp[5], w_gate_b, w_up_b, w_down_b, tm=min(TM_MOE, seq), prec=None)
        outs[2].append(lf)
        outs[3].append(st_p)
        outs[4].append(cv_p)

        qkv, z, xbc, gates, small = _norm_proj(
            ys, ms[1], ms[0], g_norm1[l], w_main, w_small, _sections(F32), tm=db, tn=IN_TILE, prec=HI)
        qn, k_new, v_new, lf_new = _qkv_post_sample(qkv, small, g_q[l], g_k[l], b_f[l])
        o_att_t = _attention_decode(l, page_table, qn.reshape(db, N_HEADS, HEAD_DIM),
                                    k_new.reshape(db, N_KV_HEADS, HEAD_DIM), v_new.reshape(db, N_KV_HEADS, HEAD_DIM),
                                    lf_new[0], cache_kt, cache_vt, cache_lft)
        o_att = jnp.swapaxes(o_att_t, 1, 2).reshape(1, db, ATT_WIDTH)
        conv_new, xd3, da, bm, cm, dx = _ssm_step_prep(
            l, xbc[0], conv_state, small[0], conv_w[l], conv_b[l], dt_bias[l], a_log[l], d_skip[l])
        st_s, y_t = _ssm_step_state(l, ssm_state, st_s, xd3, da, bm, cm)
        y_ssm = _ssm_step_gate(y_t, dx, z[0], g_ssm[l])
        x1, h2, comb = _merge(l, o_att, y_ssm[None], gates, ys, ms[2], ms[4], ms[3],
                              g_norm2[l], w_pa, w_pb, w_o, w_router, b_router, tm=db, h2_dtype=F32, prec=HI)
        ys = _moe(l, h2, comb, x1, ms[5], w_gate, w_up, w_down, tm=db, prec=HI)
        outs[5].append(k_new.reshape(db, 1, N_KV_HEADS, HEAD_DIM))
        outs[6].append(v_new.reshape(db, 1, N_KV_HEADS, HEAD_DIM))
        outs[7].append(lf_new.reshape(db, 1, N_HEADS))
        outs[8].append(conv_new)

    stk = [jnp.stack(o) if o else None for o in outs]
    k_prompt, v_prompt = (a.reshape(depth, bp, seq, N_KV_HEADS, HEAD_DIM) for a in kv_all)
    return (yp, ys.reshape(db, 1, d), k_prompt, v_prompt, stk[2], stk[3], stk[4], stk[5], stk[6], stk[7],
            st_s.reshape(depth, db, SSM_HEADS, SSM_HEAD_DIM, D_STATE), jnp.transpose(stk[8], (0, 2, 1, 3)))
```

```python
import functools

import jax
import jax.numpy as jnp
from jax import lax
from jax.experimental import pallas as pl
from jax.experimental.pallas import tpu as pltpu

F32 = jnp.float32
BF16 = jnp.bfloat16
HI = lax.Precision.HIGHEST
EPS = 1e-6
NEG = -1e30

N_HEADS = 16
N_KV_HEADS = 8
HEAD_DIM = 64
ATT_WIDTH = N_HEADS * HEAD_DIM
KV_WIDTH = N_KV_HEADS * HEAD_DIM
SSM_HEADS = 32
SSM_HEAD_DIM = 64
SSM_GROUPS = 4
D_STATE = 128
CONV_WIDTH = 4
D_INNER = SSM_HEADS * SSM_HEAD_DIM
BC_WIDTH = SSM_GROUPS * D_STATE
CONV_DIM = D_INNER + 2 * BC_WIDTH
SSD_CHUNK = 128
N_EXPERTS = 16
N_EXPERT_GROUPS = 4
EXPERTS_PER_GROUP = N_EXPERTS // N_EXPERT_GROUPS

VMEM_LIMIT_BYTES = 56 * 1024 * 1024
LANES = 128

NT_DIMS = (((1,), (1,)), ((), ()))


def _params(*sem):
    return pltpu.CompilerParams(dimension_semantics=sem, vmem_limit_bytes=VMEM_LIMIT_BYTES)


def _sigmoid(x):
    return 1.0 / (1.0 + jnp.exp(-x))


def _silu(x):
    return x * _sigmoid(x)


def _softplus(x):
    return jnp.maximum(x, 0.0) + jnp.log1p(jnp.exp(-jnp.abs(x)))


def _dot(a, b, prec=None):
    return jnp.dot(a, b, precision=prec, preferred_element_type=F32)


def _dot_nt(a, b, prec=None):
    return lax.dot_general(a, b, NT_DIMS, precision=prec, preferred_element_type=F32)


def _eye(n):
    return (lax.broadcasted_iota(jnp.int32, (n, n), 0) == lax.broadcasted_iota(jnp.int32, (n, n), 1)).astype(F32)


def _head_expander(n_heads, width):
    r = lax.broadcasted_iota(jnp.int32, (n_heads, n_heads * width), 0)
    c = lax.broadcasted_iota(jnp.int32, (n_heads, n_heads * width), 1)
    return (c // width == r).astype(F32)


def _mod_kernel(c_ref, w_ref, b_ref, o_ref):
    o_ref[0] = _dot(_silu(c_ref[...]), w_ref[0], HI) + b_ref[0]


def _modulation(c_all, w_mod, b_mod, tn=512):
    depth, d, n = w_mod.shape
    rows = c_all.shape[0]
    return pl.pallas_call(
        _mod_kernel,
        grid=(depth, n // tn),
        in_specs=[pl.BlockSpec((rows, d), lambda l, j: (0, 0)),
                  pl.BlockSpec((1, d, tn), lambda l, j: (l, 0, j)),
                  pl.BlockSpec((1, 1, tn), lambda l, j: (l, 0, j))],
        out_specs=pl.BlockSpec((1, rows, tn), lambda l, j: (l, 0, j)),
        out_shape=jax.ShapeDtypeStruct((depth, rows, n), F32),
        compiler_params=_params("parallel", "parallel"),
        name="modulation",
    )(c_all, w_mod, b_mod.reshape(depth, 1, n))


def _norm_proj_kernel(x_ref, sc_ref, sh_ref, g_ref, w_ref, ws_ref, *refs, starts, prec):
    n_sec = len(starts) - 1
    out_refs, small_ref, h_ref = refs[:n_sec], refs[n_sec], refs[n_sec + 1]
    n = pl.program_id(2)

    @pl.when(n == 0)
    def _():
        x = x_ref[0]
        y = x * lax.rsqrt(jnp.mean(x * x, axis=-1, keepdims=True) + EPS)
        h = (y * g_ref[...]) * (1.0 + sc_ref[0]) + sh_ref[0]
        h = h.astype(h_ref.dtype)
        h_ref[...] = h
        small_ref[0] = _dot(h, ws_ref[...], prec)

    for i in range(n_sec):
        @pl.when(jnp.logical_and(n >= starts[i], n < starts[i + 1]))
        def _(i=i):
            out_refs[i][0] = _dot(h_ref[...], w_ref[...], prec).astype(out_refs[i].dtype)


def _norm_proj(x, sc, sh, g, w_main, w_small, sections, *, tm, tn, prec):
    b, l, d = x.shape
    mod_rows = sc.shape[1]
    mr = tm if mod_rows == l else 1
    starts = [0]
    for nt, _ in sections:
        starts.append(starts[-1] + nt)
    starts = tuple(starts)
    n_tiles = starts[-1]
    assert w_main.shape == (d, n_tiles * tn) and l % tm == 0
    ws_cols = w_small.shape[1]

    def mod_map(bi, mi, ni):
        return (bi, mi if mod_rows == l else 0, 0)

    def sec_map(i):
        lo, cnt = starts[i], sections[i][0]
        return lambda bi, mi, ni: (bi, mi, jnp.clip(ni - lo, 0, cnt - 1))

    out_specs = [pl.BlockSpec((1, tm, tn), sec_map(i)) for i in range(len(sections))]
    out_specs.append(pl.BlockSpec((1, tm, ws_cols), lambda bi, mi, ni: (bi, mi, 0)))
    out_shape = [jax.ShapeDtypeStruct((b, l, nt * tn), dt) for nt, dt in sections]
    out_shape.append(jax.ShapeDtypeStruct((b, l, ws_cols), F32))
    return pl.pallas_call(
        functools.partial(_norm_proj_kernel, starts=starts, prec=prec),
        grid=(b, l // tm, n_tiles),
        in_specs=[pl.BlockSpec((1, tm, d), lambda bi, mi, ni: (bi, mi, 0)),
                  pl.BlockSpec((1, mr, d), mod_map),
                  pl.BlockSpec((1, mr, d), mod_map),
                  pl.BlockSpec((1, d), lambda bi, mi, ni: (0, 0)),
                  pl.BlockSpec((d, tn), lambda bi, mi, ni: (0, ni)),
                  pl.BlockSpec((d, ws_cols), lambda bi, mi, ni: (0, 0))],
        out_specs=out_specs,
        out_shape=out_shape,
        scratch_shapes=[pltpu.VMEM((tm, d), w_main.dtype)],
        compiler_params=_params("parallel", "parallel", "arbitrary"),
        name="norm_proj",
    )(x, sc, sh, g.reshape(1, d), w_main, w_small)


def _head_norm(x, g):
    return x * lax.rsqrt(jnp.mean(x * x, axis=-1, keepdims=True) + EPS) * g


def _log_forget(small_ref, bf_ref):
    fl = small_ref[0][:, :N_HEADS] + bf_ref[...]
    return -_softplus(-fl)


def _qkv_post_prompt_kernel(qkv_ref, small_ref, gqk_ref, bf_ref, *refs):
    qh_ref, kh_ref, vh_ref, ko_ref, vo_ref, lf_ref = refs[-6:]
    ko_ref, vo_ref = ko_ref.at[0], vo_ref.at[0]
    n_qk = ATT_WIDTH + KV_WIDTH
    seg = (lax.broadcasted_iota(jnp.int32, (n_qk, LANES), 0) // HEAD_DIM
           == lax.broadcasted_iota(jnp.int32, (n_qk, LANES), 1)).astype(BF16)
    seg_t = (lax.broadcasted_iota(jnp.int32, (LANES, n_qk), 1) // HEAD_DIM
             == lax.broadcasted_iota(jnp.int32, (LANES, n_qk), 0)).astype(BF16)

    def two_term(a, m):
        hi = a.astype(BF16)
        return _dot(hi, m) + _dot((a - hi.astype(F32)).astype(BF16), m)

    x = qkv_ref[0, :, :n_qk]
    rs = lax.rsqrt(two_term(x * x, seg) * (1.0 / HEAD_DIM) + EPS)
    xn = x * two_term(rs, seg_t) * gqk_ref[...]
    for h in range(N_HEADS):
        qh_ref[0, h] = xn[:, h * HEAD_DIM:(h + 1) * HEAD_DIM].astype(qh_ref.dtype)
    ko_ref[0] = xn[:, ATT_WIDTH:]
    for g in range(N_KV_HEADS):
        lo = ATT_WIDTH + g * HEAD_DIM
        kh_ref[0, g] = xn[:, lo:lo + HEAD_DIM].astype(kh_ref.dtype)
        lo = ATT_WIDTH + KV_WIDTH + g * HEAD_DIM
        v = qkv_ref[0, :, lo:lo + HEAD_DIM].astype(vh_ref.dtype)
        rep = N_HEADS // N_KV_HEADS
        for r in range(rep):
            vh_ref[0, g, :, r * HEAD_DIM:(r + 1) * HEAD_DIM] = v
        vh_ref[0, g, :, rep * HEAD_DIM:] = jnp.ones((v.shape[0], rep * HEAD_DIM), vh_ref.dtype)
    vo_ref[0] = qkv_ref[0, :, ATT_WIDTH + KV_WIDTH:ATT_WIDTH + 2 * KV_WIDTH]
    lf_ref[0] = _log_forget(small_ref, bf_ref)


def _qkv_post_prompt(layer, depth, kv_all, qkv, small, g_q, g_k, b_f, *, tm):
    b, l, _ = qkv.shape
    tok = lambda w: pl.BlockSpec((1, tm, w), lambda bi, mi: (bi, mi, 0))
    head = lambda n, w=HEAD_DIM: pl.BlockSpec((1, n, tm, w), lambda bi, mi: (bi, 0, mi, 0))
    vec = lambda w: pl.BlockSpec((1, w), lambda bi, mi: (0, 0))
    kv_out = pl.BlockSpec((1, 1, tm, KV_WIDTH), lambda bi, mi: (layer, bi, mi, 0))
    v_width = 2 * (N_HEADS // N_KV_HEADS) * HEAD_DIM
    g_qk = jnp.concatenate([jnp.tile(g_q, N_HEADS) * HEAD_DIM ** -0.5, jnp.tile(g_k, N_KV_HEADS)]).reshape(1, -1)
    in_specs = [tok(ATT_WIDTH + 2 * KV_WIDTH), tok(LANES), vec(ATT_WIDTH + KV_WIDTH), vec(N_HEADS)]
    args = [qkv, small, g_qk, b_f.reshape(1, -1)]
    aliases = {}
    if kv_all is not None:
        in_specs += [pl.BlockSpec(memory_space=pl.ANY)] * 2
        aliases = {len(args): 3, len(args) + 1: 4}
        args += list(kv_all)
    return pl.pallas_call(
        _qkv_post_prompt_kernel,
        grid=(b, l // tm),
        in_specs=in_specs,
        out_specs=[head(N_HEADS), head(N_KV_HEADS), head(N_KV_HEADS, v_width), kv_out, kv_out, tok(N_HEADS)],
        out_shape=[jax.ShapeDtypeStruct((b, N_HEADS, l, HEAD_DIM), BF16),
                   jax.ShapeDtypeStruct((b, N_KV_HEADS, l, HEAD_DIM), BF16),
                   jax.ShapeDtypeStruct((b, N_KV_HEADS, l, v_width), BF16),
                   jax.ShapeDtypeStruct((depth, b, l, KV_WIDTH), F32),
                   jax.ShapeDtypeStruct((depth, b, l, KV_WIDTH), F32),
                   jax.ShapeDtypeStruct((b, l, N_HEADS), F32)],
        input_output_aliases=aliases,
        compiler_params=_params("parallel", "parallel"),
        name="qkv_post_prompt",
    )(*args)


def _qkv_post_sample_kernel(qkv_ref, small_ref, gq_ref, gk_ref, bf_ref, qo_ref, ko_ref, vo_ref, lf_ref):
    scale = HEAD_DIM ** -0.5
    for h in range(N_HEADS):
        sl = slice(h * HEAD_DIM, (h + 1) * HEAD_DIM)
        qo_ref[0, :, sl] = _head_norm(qkv_ref[0, :, sl], gq_ref[...]) * scale
    for g in range(N_KV_HEADS):
        lo = ATT_WIDTH + g * HEAD_DIM
        ko_ref[0, :, g * HEAD_DIM:(g + 1) * HEAD_DIM] = _head_norm(qkv_ref[0, :, lo:lo + HEAD_DIM], gk_ref[...])
    vo_ref[0] = qkv_ref[0, :, ATT_WIDTH + KV_WIDTH:ATT_WIDTH + 2 * KV_WIDTH]
    lf_ref[0] = _log_forget(small_ref, bf_ref)


def _qkv_post_sample(qkv, small, g_q, g_k, b_f):
    b, l, _ = qkv.shape
    tok = lambda w: pl.BlockSpec((1, l, w), lambda bi: (bi, 0, 0))
    vec = lambda w: pl.BlockSpec((1, w), lambda bi: (0, 0))
    return pl.pallas_call(
        _qkv_post_sample_kernel,
        grid=(b,),
        in_specs=[tok(ATT_WIDTH + 2 * KV_WIDTH), tok(LANES), vec(HEAD_DIM), vec(HEAD_DIM), vec(N_HEADS)],
        out_specs=[tok(ATT_WIDTH), tok(KV_WIDTH), tok(KV_WIDTH), tok(N_HEADS)],
        out_shape=[jax.ShapeDtypeStruct((b, l, ATT_WIDTH), F32),
                   jax.ShapeDtypeStruct((b, l, KV_WIDTH), F32),
                   jax.ShapeDtypeStruct((b, l, KV_WIDTH), F32),
                   jax.ShapeDtypeStruct((b, l, N_HEADS), F32)],
        compiler_params=_params("parallel"),
        name="qkv_post_sample",
    )(qkv, small, g_q.reshape(1, -1), g_k.reshape(1, -1), b_f.reshape(1, -1))


def _cumsum_kernel(lf_ref, cf_ref, cft_ref, *, chunk):
    l = lf_ref.shape[1]
    r = lax.broadcasted_iota(jnp.int32, (chunk, chunk), 0)
    c = lax.broadcasted_iota(jnp.int32, (chunk, chunk), 1)
    tril = (r >= c).astype(F32)
    eye = _eye(N_HEADS)
    carry = jnp.zeros((1, N_HEADS), F32)
    for i in range(l // chunk):
        sl = slice(i * chunk, (i + 1) * chunk)
        loc = _dot(tril, lf_ref[0, sl, :], HI) + carry
        cf_ref[0, sl, :] = loc
        cft_ref[0, :, sl] = _dot_nt(eye, loc, HI)
        carry = loc[chunk - 1:chunk, :]


def _cumsum_logf(lf, *, chunk=256):
    b, l, h = lf.shape
    return pl.pallas_call(
        functools.partial(_cumsum_kernel, chunk=chunk),
        grid=(b,),
        in_specs=[pl.BlockSpec((1, l, h), lambda bi: (bi, 0, 0))],
        out_specs=[pl.BlockSpec((1, l, h), lambda bi: (bi, 0, 0)), pl.BlockSpec((1, h, l), lambda bi: (bi, 0, 0))],
        out_shape=[jax.ShapeDtypeStruct((b, l, h), F32), jax.ShapeDtypeStruct((b, h, l), F32)],
        compiler_params=_params("parallel"),
        name="cumsum_logf",
    )(lf)


def _attn_prompt_kernel(q_ref, k_ref, v_ref, cf_ref, cft_ref, o_ref, m_ref, l_ref, acc_ref, *, tq):
    g, qi = pl.program_id(1), pl.program_id(2)
    rep = N_HEADS // N_KV_HEADS
    tk = tq
    width = rep * HEAD_DIM
    lane16 = lax.broadcasted_iota(jnp.int32, (tq, N_HEADS), 1)
    cfb = cf_ref[0]
    cq = [jnp.broadcast_to(jnp.sum(jnp.where(lane16 == rep * g + r, cfb, 0.0), axis=-1, keepdims=True), (tq, width))
          for r in range(rep)]
    head_of_lane = lax.broadcasted_iota(jnp.int32, (tq, width), 1) // HEAD_DIM

    def per_head(vals):
        out = vals[0]
        for r in range(1, rep):
            out = jnp.where(head_of_lane == r, vals[r], out)
        return out

    m_ref[...] = jnp.full(m_ref.shape, NEG, F32)
    l_ref[...] = jnp.zeros(l_ref.shape, F32)
    acc_ref[...] = jnp.zeros(acc_ref.shape, F32)

    def block(ki, diagonal):
        keys = pl.ds(pl.multiple_of(ki * tk, tk), tk)
        alphas, pvs = [], []
        for r in range(rep):
            t = _dot_nt(q_ref[0, r], k_ref[0, 0, keys, :]) - cft_ref[0, 0, ki, r:r + 1, :]
            if diagonal:
                row = lax.broadcasted_iota(jnp.int32, (tq, tk), 0)
                col = lax.broadcasted_iota(jnp.int32, (tq, tk), 1)
                t = jnp.where(col <= row, t, NEG)
            m_prev = m_ref[r]
            m_new = jnp.maximum(m_prev, jnp.max(t, axis=-1, keepdims=True) + cq[r])
            alpha = jnp.exp(m_prev - m_new)
            p = jnp.exp(t + jnp.tile(cq[r] - m_new, (1, tk // width)))
            pv = _dot(p.astype(v_ref.dtype), v_ref[0, 0, keys, :])
            l_ref[r] = alpha * l_ref[r] + pv[:, width:]
            m_ref[r] = m_new
            alphas.append(alpha)
            pvs.append(pv[:, :width])
        acc_ref[...] = per_head(alphas) * acc_ref[...] + per_head(pvs)

    def body(ki, carry):
        block(ki, False)
        return carry

    lax.fori_loop(0, qi, body, 0)
    block(qi, True)
    o_ref[0] = (acc_ref[...] / per_head([l_ref[r] for r in range(rep)])).astype(o_ref.dtype)


def _attention_prompt(qh, kh, vh_ext, cf, cft, *, tq):
    b, _, l, _ = qh.shape
    rep = N_HEADS // N_KV_HEADS
    width = rep * HEAD_DIM
    nq = l // tq
    cft_tiles = cft.reshape(b, N_KV_HEADS, rep, nq, tq).transpose(0, 1, 3, 2, 4)
    return pl.pallas_call(
        functools.partial(_attn_prompt_kernel, tq=tq),
        grid=(b, N_KV_HEADS, nq),
        in_specs=[pl.BlockSpec((1, rep, tq, HEAD_DIM), lambda bi, g, qi: (bi, g, qi, 0)),
                  pl.BlockSpec((1, 1, l, HEAD_DIM), lambda bi, g, qi: (bi, g, 0, 0)),
                  pl.BlockSpec((1, 1, l, 2 * width), lambda bi, g, qi: (bi, g, 0, 0)),
                  pl.BlockSpec((1, tq, N_HEADS), lambda bi, g, qi: (bi, qi, 0)),
                  pl.BlockSpec((1, 1, nq, rep, tq), lambda bi, g, qi: (bi, g, 0, 0, 0))],
        out_specs=pl.BlockSpec((1, tq, width), lambda bi, g, qi: (bi, qi, g)),
        out_shape=jax.ShapeDtypeStruct((b, l, ATT_WIDTH), BF16),
        scratch_shapes=[pltpu.VMEM((rep, tq, width), F32), pltpu.VMEM((rep, tq, width), F32),
                        pltpu.VMEM((tq, width), F32)],
        compiler_params=_params("parallel", "parallel", "arbitrary"),
        name="attention_prompt",
    )(qh, kh, vh_ext, cf, cft_tiles)


def _attn_decode_kernel(pt_ref, qt_ref, q_ref, kn_ref, vnt_ref, lfn_ref, *refs, n_pages, page):
    del pt_ref
    k_refs, v_refs, lf_refs = refs[:n_pages], refs[n_pages:2 * n_pages], refs[2 * n_pages:3 * n_pages]
    ot_ref, qb_ref, s_ref = refs[3 * n_pages:]
    rep = N_HEADS // N_KV_HEADS

    hrow = lax.broadcasted_iota(jnp.int32, (N_HEADS, N_HEADS * page), 0)
    hcol = lax.broadcasted_iota(jnp.int32, (N_HEADS, N_HEADS * page), 1) // page
    q_all = _dot(qt_ref[0], (hrow == hcol).astype(F32), HI)
    for h in range(N_HEADS):
        qb_ref[h] = q_all[:, h * page:(h + 1) * page]

    j = lax.broadcasted_iota(jnp.int32, (page, page), 0)
    t = lax.broadcasted_iota(jnp.int32, (page, page), 1)
    later = (j > t).astype(F32)
    totals = [jnp.sum(lf_refs[p][0, 0], axis=-1, keepdims=True) for p in range(n_pages)]
    carry = lfn_ref[0]
    for p in reversed(range(n_pages)):
        s_ref[:, p * page:(p + 1) * page] = _dot(lf_refs[p][0, 0], later, HI) + carry
        carry = carry + totals[p]

    for p in range(n_pages):
        cols = slice(p * page, (p + 1) * page)
        for g in range(N_KV_HEADS):
            kt = k_refs[p][0, 0, g]
            for r in range(rep):
                h = g * rep + r
                s_ref[h:h + 1, cols] = s_ref[h:h + 1, cols] + jnp.sum(kt * qb_ref[h], axis=0, keepdims=True)

    s = s_ref[...]
    s_new = jnp.sum(q_ref[0] * kn_ref[0], axis=-1, keepdims=True)
    m = jnp.maximum(jnp.max(s, axis=-1, keepdims=True), s_new)
    pmat = jnp.exp(s - m)
    e_new = jnp.exp(s_new - m)
    denom = jnp.sum(pmat, axis=-1, keepdims=True) + e_new
    s_ref[...] = pmat

    lane = lax.broadcasted_iota(jnp.int32, (HEAD_DIM, N_HEADS), 1)
    ot = jnp.zeros((HEAD_DIM, N_HEADS), F32)
    for g in range(N_KV_HEADS):
        acc = [jnp.zeros((HEAD_DIM, page), F32) for _ in range(rep)]
        for p in range(n_pages):
            vt = v_refs[p][0, 0, g]
            for r in range(rep):
                h = g * rep + r
                acc[r] = acc[r] + vt * s_ref[h:h + 1, p * page:(p + 1) * page]
        for r in range(rep):
            ot = jnp.where(lane == g * rep + r, jnp.sum(acc[r], axis=-1, keepdims=True), ot)
    eye = _eye(N_HEADS)
    e_row = jnp.sum(eye * e_new, axis=0, keepdims=True)
    d_row = jnp.sum(eye * denom, axis=0, keepdims=True)
    ot_ref[0] = (ot + vnt_ref[0] * e_row) / d_row


def _attention_decode(layer, page_table, q, k_new, v_new, lf_new, cache_kt, cache_vt, cache_lft):
    db, n_pages = page_table.shape
    page = cache_kt.shape[-1]
    rep = N_HEADS // N_KV_HEADS
    qt = jnp.swapaxes(q, 1, 2)
    kn = jnp.repeat(k_new, rep, axis=1)
    vnt = jnp.swapaxes(jnp.repeat(v_new, rep, axis=1), 1, 2)
    rows = lambda: pl.BlockSpec((1, N_HEADS, HEAD_DIM), lambda bi, pt: (bi, 0, 0))
    cols = lambda: pl.BlockSpec((1, HEAD_DIM, N_HEADS), lambda bi, pt: (bi, 0, 0))
    kv_spec = lambda p: pl.BlockSpec((1, 1, N_KV_HEADS, HEAD_DIM, page),
                                     lambda bi, pt: (layer, pt[bi, p], 0, 0, 0))
    lf_spec = lambda p: pl.BlockSpec((1, 1, N_HEADS, page), lambda bi, pt: (layer, pt[bi, p], 0, 0))
    pages = range(n_pages)
    grid_spec = pltpu.PrefetchScalarGridSpec(
        num_scalar_prefetch=1,
        grid=(db,),
        in_specs=[cols(), rows(), rows(), cols(), pl.BlockSpec((1, N_HEADS, 1), lambda bi, pt: (bi, 0, 0))]
        + [kv_spec(p) for p in pages] + [kv_spec(p) for p in pages] + [lf_spec(p) for p in pages],
        out_specs=cols(),
        scratch_shapes=[pltpu.VMEM((N_HEADS, HEAD_DIM, page), F32), pltpu.VMEM((N_HEADS, n_pages * page), F32)])
    return pl.pallas_call(
        functools.partial(_attn_decode_kernel, n_pages=n_pages, page=page),
        grid_spec=grid_spec,
        out_shape=jax.ShapeDtypeStruct((db, HEAD_DIM, N_HEADS), F32),
        compiler_params=_params("parallel"),
        name="attention_decode",
    )(page_table, qt, q, kn, vnt, lf_new[:, :, None],
      *([cache_kt] * n_pages), *([cache_vt] * n_pages), *([cache_lft] * n_pages))


def _ssd_prompt_kernel(xbc_ref, z_ref, dt_ref, cw_ref, cb_ref, dtb_ref, alog_ref, dskip_ref, gssm_ref,
                       y_ref, st_ref, cv_ref, ext_ref, state_ref, yd_ref):
    c = pl.program_id(1)
    t = SSD_CHUNK
    hist = 8
    hpg = SSM_HEADS // SSM_GROUPS
    gw = hpg * SSM_HEAD_DIM

    @pl.when(c == 0)
    def _():
        ext_ref[0:hist, :] = jnp.zeros((hist, CONV_DIM), F32)
        state_ref[...] = jnp.zeros(state_ref.shape, F32)

    ext_ref[hist:hist + t, :] = xbc_ref[0].astype(F32)
    conv = cb_ref[...] + ext_ref[hist:hist + t, :] * cw_ref[CONV_WIDTH - 1:CONV_WIDTH, :]
    for j in range(1, CONV_WIDTH):
        conv = conv + ext_ref[hist - j:hist - j + t, :] * cw_ref[CONV_WIDTH - 1 - j:CONV_WIDTH - j, :]
    tail = ext_ref[t:t + hist, :]
    cv_ref[0] = tail[hist - (CONV_WIDTH - 1):, :]
    ext_ref[0:hist, :] = tail
    u = _silu(conv)
    xs = u[:, :D_INNER]

    dt = _softplus(dt_ref[0][:, :SSM_HEADS] + dtb_ref[...])
    a = -jnp.exp(alog_ref[...])
    r = lax.broadcasted_iota(jnp.int32, (t, t), 0)
    cidx = lax.broadcasted_iota(jnp.int32, (t, t), 1)
    tri = r >= cidx
    acs = _dot(tri.astype(F32), dt * a, HI)
    acs_t = _dot_nt(_eye(SSM_HEADS), acs, HI)
    expand = _head_expander(SSM_HEADS, SSM_HEAD_DIM).astype(BF16)

    def per_channel(x):
        hi = x.astype(BF16)
        return _dot(hi, expand) + _dot((x - hi.astype(F32)).astype(BF16), expand)

    dt_x = per_channel(dt)
    ea_x = per_channel(jnp.exp(acs))
    dec_x = per_channel(jnp.exp(acs[t - 1:t, :] - acs))
    xd = xs * dt_x
    xdd = (xd * dec_x).astype(BF16)
    xd_b = xd.astype(BF16)
    cdec_x = ea_x[t - 1:t, :]

    for g in range(SSM_GROUPS):
        bg = u[:, D_INNER + g * D_STATE:D_INNER + (g + 1) * D_STATE]
        cg = u[:, D_INNER + BC_WIDTH + g * D_STATE:D_INNER + BC_WIDTH + (g + 1) * D_STATE]
        bg_b, cg_b = bg.astype(BF16), cg.astype(BF16)
        cb = _dot_nt(cg_b, bg_b)
        gs = slice(g * gw, (g + 1) * gw)
        s_prev = state_ref[g]
        y_off = _dot(cg_b, s_prev.astype(BF16)) * ea_x[:, gs]
        state_ref[g] = s_prev * cdec_x[:, gs] + _dot(bg.T.astype(BF16), xdd[:, gs])
        for hh in range(hpg):
            h = g * hpg + hh
            seg = acs[:, h:h + 1] - acs_t[h:h + 1, :]
            lmat = jnp.exp(jnp.where(tri, seg, NEG))
            hs = slice(h * SSM_HEAD_DIM, (h + 1) * SSM_HEAD_DIM)
            yd_ref[:, hs] = _dot((cb * lmat).astype(BF16), xd_b[:, hs])
        yd_ref[:, gs] = yd_ref[:, gs] + y_off

    y = (yd_ref[...] + dskip_ref[...] * xs) * _silu(z_ref[0].astype(F32))
    for g in range(SSM_GROUPS):
        gs = slice(g * gw, (g + 1) * gw)
        yg = y[:, gs]
        yg = yg * lax.rsqrt(jnp.mean(yg * yg, axis=-1, keepdims=True) + EPS)
        y_ref[0, :, gs] = (yg * gssm_ref[:, gs]).astype(y_ref.dtype)

    @pl.when(c == pl.num_programs(1) - 1)
    def _():
        for g in range(SSM_GROUPS):
            st_ref[0, g * gw:(g + 1) * gw, :] = state_ref[g].T


def _ssd_prompt(xbc, z, dtr, conv_w, conv_b, dt_bias, a_log, d_skip, g_ssm):
    b, l, _ = xbc.shape
    t = SSD_CHUNK
    gw = D_INNER // SSM_GROUPS
    vec = lambda w: pl.BlockSpec((1, w), lambda bi, ci: (0, 0))
    y, st, cv = pl.pallas_call(
        _ssd_prompt_kernel,
        grid=(b, l // t),
        in_specs=[pl.BlockSpec((1, t, CONV_DIM), lambda bi, ci: (bi, ci, 0)),
                  pl.BlockSpec((1, t, D_INNER), lambda bi, ci: (bi, ci, 0)),
                  pl.BlockSpec((1, t, LANES), lambda bi, ci: (bi, ci, 1)),
                  pl.BlockSpec((CONV_WIDTH, CONV_DIM), lambda bi, ci: (0, 0)),
                  vec(CONV_DIM), vec(SSM_HEADS), vec(SSM_HEADS), vec(D_INNER), vec(D_INNER)],
        out_specs=[pl.BlockSpec((1, t, D_INNER), lambda bi, ci: (bi, ci, 0)),
                   pl.BlockSpec((1, D_INNER, D_STATE), lambda bi, ci: (bi, 0, 0)),
                   pl.BlockSpec((1, CONV_WIDTH - 1, CONV_DIM), lambda bi, ci: (bi, 0, 0))],
        out_shape=[jax.ShapeDtypeStruct((b, l, D_INNER), BF16),
                   jax.ShapeDtypeStruct((b, D_INNER, D_STATE), F32),
                   jax.ShapeDtypeStruct((b, CONV_WIDTH - 1, CONV_DIM), F32)],
        scratch_shapes=[pltpu.VMEM((t + 8, CONV_DIM), F32),
                        pltpu.VMEM((SSM_GROUPS, D_STATE, gw), F32),
                        pltpu.VMEM((t, D_INNER), F32)],
        compiler_params=_params("parallel", "arbitrary"),
        name="ssd_prompt",
    )(xbc, z, dtr, conv_w, conv_b.reshape(1, -1), dt_bias.reshape(1, -1), a_log.reshape(1, -1),
      jnp.repeat(d_skip, SSM_HEAD_DIM).reshape(1, -1), g_ssm.reshape(1, -1))
    return y, st.reshape(b, SSM_HEADS, SSM_HEAD_DIM, D_STATE), cv


def _ssm_step_prep_kernel(xbc_ref, conv0_ref, dt_ref, cw_ref, cb_ref, dtb_ref, alog_ref, dskip_ref,
                          convn_ref, xd3_ref, da_ref, b_ref, c_ref, dx_ref):
    w = CONV_WIDTH
    db = xbc_ref.shape[0]
    xbc = xbc_ref[...]
    conv = cb_ref[...] + xbc * cw_ref[w - 1:w, :]
    for i in range(w - 1):
        conv = conv + conv0_ref[0, i] * cw_ref[i:i + 1, :]
    for i in range(w - 2):
        convn_ref[i] = conv0_ref[0, i + 1]
    convn_ref[w - 2] = xbc
    u = _silu(conv)
    xs = u[:, :D_INNER]
    b_ref[...] = u[:, D_INNER:D_INNER + BC_WIDTH]
    c_ref[...] = u[:, D_INNER + BC_WIDTH:]
    dt = _softplus(dt_ref[...][:, :SSM_HEADS] + dtb_ref[...])
    da_ref[...] = jnp.exp(dt * -jnp.exp(alog_ref[...]))
    xd = xs * _dot(dt, _head_expander(SSM_HEADS, SSM_HEAD_DIM), HI)
    dx_ref[...] = dskip_ref[...] * xs
    for j in range(D_INNER // LANES):
        sl = slice(j * LANES, (j + 1) * LANES)
        xt = xd[:, sl].T
        hi = xt.astype(BF16)
        rest = xt - hi.astype(F32)
        mid = rest.astype(BF16)
        xd3_ref[sl, 0:db] = hi
        xd3_ref[sl, db:2 * db] = mid
        xd3_ref[sl, 2 * db:3 * db] = (rest - mid.astype(F32)).astype(BF16)


def _ssm_step_prep(layer, xbc, conv_state, dtr, conv_w, conv_b, dt_bias, a_log, d_skip):
    db = xbc.shape[0]
    assert db == LANES
    full = lambda *s: pl.BlockSpec(s, lambda i: tuple(0 for _ in s))
    hist = CONV_WIDTH - 1
    return pl.pallas_call(
        _ssm_step_prep_kernel,
        grid=(1,),
        in_specs=[full(db, CONV_DIM), pl.BlockSpec((1, hist, db, CONV_DIM), lambda i: (layer, 0, 0, 0)),
                  pl.BlockSpec((db, LANES), lambda i: (0, 1)),
                  full(CONV_WIDTH, CONV_DIM), full(1, CONV_DIM), full(1, SSM_HEADS), full(1, SSM_HEADS),
                  full(1, D_INNER)],
        out_specs=[full(hist, db, CONV_DIM), full(D_INNER, 3 * db), full(db, SSM_HEADS), full(db, BC_WIDTH),
                   full(db, BC_WIDTH), full(db, D_INNER)],
        out_shape=[jax.ShapeDtypeStruct((hist, db, CONV_DIM), F32),
                   jax.ShapeDtypeStruct((D_INNER, 3 * db), BF16),
                   jax.ShapeDtypeStruct((db, SSM_HEADS), F32),
                   jax.ShapeDtypeStruct((db, BC_WIDTH), F32),
                   jax.ShapeDtypeStruct((db, BC_WIDTH), F32),
                   jax.ShapeDtypeStruct((db, D_INNER), F32)],
        compiler_params=_params("arbitrary"),
        name="ssm_step_prep",
    )(xbc, conv_state, dtr, conv_w, conv_b.reshape(1, -1), dt_bias.reshape(1, -1), a_log.reshape(1, -1),
      jnp.repeat(d_skip, SSM_HEAD_DIM).reshape(1, -1))


def _ssm_step_state_kernel(da_ref, s_ref, xd3_ref, b_ref, c_ref, *refs):
    so_ref, yt_ref = refs[-2:]
    s_ref, so_ref = s_ref.at[0], so_ref.at[0]
    bi = pl.program_id(0)
    db = xd3_ref.shape[1] // 3
    hpg = SSM_HEADS // SSM_GROUPS

    @pl.when(bi == 0)
    def _():
        yt_ref[...] = jnp.zeros(yt_ref.shape, F32)

    pick = lax.broadcasted_iota(jnp.int32, (3 * db, D_STATE), 0) % db == bi
    xcol = _dot(xd3_ref[...], jnp.where(pick, 1.0, 0.0).astype(BF16))
    lane = lax.broadcasted_iota(jnp.int32, (SSM_HEAD_DIM, db), 1)
    for h in range(SSM_HEADS):
        rows = slice(h * SSM_HEAD_DIM, (h + 1) * SSM_HEAD_DIM)
        ns = slice((h // hpg) * D_STATE, (h // hpg + 1) * D_STATE)
        s_new = s_ref[0, rows, :] * da_ref[bi, h] + xcol[rows, :] * b_ref[0, :, ns]
        so_ref[0, rows, :] = s_new
        y_col = jnp.sum(s_new * c_ref[0, :, ns], axis=-1, keepdims=True)
        yt_ref[rows, :] = jnp.where(lane == bi, y_col, yt_ref[rows, :])


def _ssm_step_state(layer, state_all, new_state_all, xd3, da, bm, cm):
    depth, db, rows, _ = state_all.shape
    seq3 = lambda w: pl.BlockSpec((1, 1, w), lambda bi: (bi, 0, 0))
    st_spec = pl.BlockSpec((1, 1, rows, D_STATE), lambda bi: (layer, bi, 0, 0))
    in_specs = [pl.BlockSpec(memory_space=pltpu.SMEM), st_spec, pl.BlockSpec((rows, 3 * db), lambda bi: (0, 0)),
                seq3(BC_WIDTH), seq3(BC_WIDTH)]
    args = [da, state_all, xd3, bm[:, None, :], cm[:, None, :]]
    aliases = {}
    if new_state_all is not None:
        in_specs.append(pl.BlockSpec(memory_space=pl.ANY))
        args.append(new_state_all)
        aliases = {len(args) - 1: 0}
    return pl.pallas_call(
        _ssm_step_state_kernel,
        grid=(db,),
        in_specs=in_specs,
        out_specs=[st_spec, pl.BlockSpec((rows, db), lambda bi: (0, 0))],
        out_shape=[jax.ShapeDtypeStruct(state_all.shape, F32), jax.ShapeDtypeStruct((rows, db), F32)],
        input_output_aliases=aliases,
        compiler_params=_params("arbitrary"),
        name="ssm_step_state",
    )(*args)


def _ssm_step_gate_kernel(yt_ref, dx_ref, z_ref, gssm_ref, o_ref):
    gw = D_INNER // SSM_GROUPS
    for g in range(SSM_GROUPS):
        gs = slice(g * gw, (g + 1) * gw)
        y = jnp.concatenate([yt_ref[g * gw + j * LANES:g * gw + (j + 1) * LANES, :].T for j in range(gw // LANES)],
                            axis=-1)
        yg = (y + dx_ref[:, gs]) * _silu(z_ref[:, gs])
        o_ref[:, gs] = yg * lax.rsqrt(jnp.mean(yg * yg, axis=-1, keepdims=True) + EPS) * gssm_ref[:, gs]


def _ssm_step_gate(yt, dx, z, g_ssm):
    db = dx.shape[0]
    assert db == LANES
    full = lambda *s: pl.BlockSpec(s, lambda i: tuple(0 for _ in s))
    return pl.pallas_call(
        _ssm_step_gate_kernel,
        grid=(1,),
        in_specs=[full(D_INNER, db), full(db, D_INNER), full(db, D_INNER), full(1, D_INNER)],
        out_specs=full(db, D_INNER),
        out_shape=jax.ShapeDtypeStruct((db, D_INNER), F32),
        compiler_params=_params("arbitrary"),
        name="ssm_step_gate",
    )(yt, dx, z, g_ssm.reshape(1, -1))


def _route(logits):
    e = jnp.exp(logits - jnp.max(logits, axis=-1, keepdims=True))
    probs = e / jnp.sum(e, axis=-1, keepdims=True)
    lane_i = lax.broadcasted_iota(jnp.int32, probs.shape, 1)
    lane = lane_i.astype(F32)
    grp = (lane_i // EXPERTS_PER_GROUP).astype(F32)
    best = jnp.max(jnp.where(grp == 0.0, probs, -1.0), axis=-1, keepdims=True)
    sel = jnp.zeros(best.shape, F32)
    for g in range(1, N_EXPERT_GROUPS):
        gmax = jnp.max(jnp.where(grp == float(g), probs, -1.0), axis=-1, keepdims=True)
        sel = jnp.where(gmax > best, float(g), sel)
        best = jnp.maximum(best, gmax)
    pin = jnp.where(grp == sel, probs, -1.0)
    v1 = jnp.max(pin, axis=-1, keepdims=True)
    i1 = jnp.min(jnp.where(pin == v1, lane, float(N_EXPERTS)), axis=-1, keepdims=True)
    pin2 = jnp.where(lane == i1, -1.0, pin)
    v2 = jnp.max(pin2, axis=-1, keepdims=True)
    i2 = jnp.min(jnp.where(pin2 == v2, lane, float(N_EXPERTS)), axis=-1, keepdims=True)
    tot = v1 + v2
    return jnp.where(lane == i1, v1 / tot, 0.0) + jnp.where(lane == i2, v2 / tot, 0.0)


MERGE_ROW_PARTS = 2
MERGE_MIN_PART_ROWS = 256


def _merge_kernel(oa_ref, ys_ref, ga_ref, gb_ref, x_ref, gt_ref, sc_ref, sh_ref, g2_ref,
                  wpa_ref, wpb_ref, wo_ref, wr_ref, br_ref, x1_ref, h2_ref, comb_ref, *, prec):
    cdt = wpa_ref.dtype
    tm = x_ref.shape[1]
    n_parts = MERGE_ROW_PARTS if tm >= MERGE_ROW_PARTS * MERGE_MIN_PART_ROWS else 1
    part = tm // n_parts
    for i in range(n_parts):
        rows = slice(i * part, (i + 1) * part)
        mrows = rows if gt_ref.shape[1] == tm else slice(None)
        pa = _dot(oa_ref[0, rows, :].astype(cdt), wpa_ref[0], prec)
        pb = _dot(ys_ref[0, rows, :].astype(cdt), wpb_ref[0], prec)
        m = _sigmoid(ga_ref[0, rows, :].astype(F32)) * pa + _sigmoid(gb_ref[0, rows, :].astype(F32)) * pb
        x1 = x_ref[0, rows, :] + gt_ref[0, mrows, :] * _dot(m.astype(cdt), wo_ref[0], prec)
        x1_ref[0, rows, :] = x1
        y = x1 * lax.rsqrt(jnp.mean(x1 * x1, axis=-1, keepdims=True) + EPS)
        h2 = (y * g2_ref[...]) * (1.0 + sc_ref[0, mrows, :]) + sh_ref[0, mrows, :]
        h2_ref[0, rows, :] = h2.astype(h2_ref.dtype)
        comb_ref[0, rows, :] = _route(_dot(h2, wr_ref[...], HI) + br_ref[...])


def _merge(layer, o_att, y_ssm, gates, x, gt1, sc2, sh2, g2, w_pa, w_pb, w_o, w_router, b_router, *,
           tm, h2_dtype, prec):
    b, l, d = x.shape
    mod_rows = gt1.shape[1]
    mr = tm if mod_rows == l else 1
    tok = lambda w, j=0: pl.BlockSpec((1, tm, w), lambda bi, mi: (bi, mi, j))
    mod = pl.BlockSpec((1, mr, d), lambda bi, mi: (bi, mi if mod_rows == l else 0, 0))
    full = lambda a: pl.BlockSpec(a.shape, lambda bi, mi: tuple(0 for _ in a.shape))
    per_layer = lambda a: pl.BlockSpec((1,) + a.shape[1:], lambda bi, mi: (layer, 0, 0))
    g2r, brr = g2.reshape(1, d), b_router.reshape(1, -1)
    return pl.pallas_call(
        functools.partial(_merge_kernel, prec=prec),
        grid=(b, l // tm),
        in_specs=[tok(ATT_WIDTH), tok(D_INNER), tok(d, 0), tok(d, 1), tok(d), mod, mod, mod, full(g2r),
                  per_layer(w_pa), per_layer(w_pb), per_layer(w_o), full(w_router), full(brr)],
        out_specs=[tok(d), tok(d), tok(N_EXPERTS)],
        out_shape=[jax.ShapeDtypeStruct((b, l, d), F32),
                   jax.ShapeDtypeStruct((b, l, d), h2_dtype),
                   jax.ShapeDtypeStruct((b, l, N_EXPERTS), F32)],
        compiler_params=_params("parallel", "parallel"),
        name="merge",
    )(o_att, y_ssm, gates, gates, x, gt1, sc2, sh2, g2r, w_pa, w_pb, w_o, w_router, brr)


def _moe_kernel(h_ref, comb_ref, x_ref, gt_ref, wg_ref, wu_ref, wd_ref, o_ref, acc_ref, *, prec):
    e = pl.program_id(2)

    @pl.when(e == 0)
    def _():
        acc_ref[...] = jnp.zeros(acc_ref.shape, F32)

    h = h_ref[0]
    comb = comb_ref[0]
    lane = lax.broadcasted_iota(jnp.int32, comb.shape, 1)
    per_step = wg_ref.shape[1]
    acts = []
    for j in range(per_step):
        w_e = jnp.sum(jnp.where(lane == e * per_step + j, comb, 0.0), axis=-1, keepdims=True)
        act = _silu(_dot(h, wg_ref[0, j], prec)) * _dot(h, wu_ref[0, j], prec) * w_e
        acts.append(act.astype(wd_ref.dtype))
    dff, d = wd_ref.shape[2], wd_ref.shape[3]
    acc_ref[...] += _dot(jnp.concatenate(acts, axis=-1), wd_ref[0].reshape(per_step * dff, d), prec)

    @pl.when(e == pl.num_programs(2) - 1)
    def _():
        o_ref[0] = x_ref[0] + gt_ref[0] * acc_ref[...]


def _moe(layer, h2, comb, x1, gt2, w_gate, w_up, w_down, *, tm, prec):
    b, l, d = x1.shape
    _, ne, _, dff = w_gate.shape
    mod_rows = gt2.shape[1]
    mr = tm if mod_rows == l else 1
    tok = lambda w: pl.BlockSpec((1, tm, w), lambda bi, mi, e: (bi, mi, 0))
    eps = MOE_EXPERTS_PER_STEP if w_gate.dtype == BF16 else MOE_EXPERTS_PER_STEP_F32
    return pl.pallas_call(
        functools.partial(_moe_kernel, prec=prec),
        grid=(b, l // tm, ne // eps),
        in_specs=[tok(d), tok(N_EXPERTS), tok(d),
                  pl.BlockSpec((1, mr, d), lambda bi, mi, e: (bi, mi if mod_rows == l else 0, 0)),
                  pl.BlockSpec((1, eps, d, dff), lambda bi, mi, e: (layer, e, 0, 0)),
                  pl.BlockSpec((1, eps, d, dff), lambda bi, mi, e: (layer, e, 0, 0)),
                  pl.BlockSpec((1, eps, dff, d), lambda bi, mi, e: (layer, e, 0, 0))],
        out_specs=tok(d),
        out_shape=jax.ShapeDtypeStruct((b, l, d), F32),
        scratch_shapes=[pltpu.VMEM((tm, d), F32)],
        compiler_params=_params("parallel", "parallel", "arbitrary"),
        name="moe",
    )(h2, comb, x1, gt2, w_gate, w_up, w_down)


TM_PROJ = 2048
TM_POST = 256
TQ_ATTN = 512
TM_MERGE = 512
TM_MOE = 512
MOE_EXPERTS_PER_STEP = 4
MOE_EXPERTS_PER_STEP_F32 = 2
IN_TILE = 512


def _split_w_in(w):
    o_q, o_k, o_v = 0, ATT_WIDTH, ATT_WIDTH + KV_WIDTH
    o_f = o_v + KV_WIDTH
    o_z = o_f + N_HEADS
    o_x = o_z + D_INNER
    o_dt = o_x + CONV_DIM
    o_ga = o_dt + SSM_HEADS
    d_model = w.shape[0]
    main = jnp.concatenate([w[:, o_q:o_f], w[:, o_z:o_x], w[:, o_x:o_dt], w[:, o_ga:]], axis=1)
    small = jnp.zeros((d_model, 2 * LANES), w.dtype)
    small = small.at[:, :N_HEADS].set(w[:, o_f:o_z]).at[:, LANES:LANES + SSM_HEADS].set(w[:, o_dt:o_ga])
    return main, small


def _sections(rest_dtype):
    qkv = (ATT_WIDTH + 2 * KV_WIDTH) // IN_TILE
    return ((qkv, F32), (D_INNER // IN_TILE, rest_dtype), (CONV_DIM // IN_TILE, rest_dtype),
            (2 * ATT_WIDTH // IN_TILE, rest_dtype))


def kernel(x_prompt, x_sample, cache_k, cache_v, cache_logf, state_ssm, state_conv, page_table, c_prompt, c_sample,
           w_mod, b_mod, g_norm1, g_norm2, w_in, b_f, g_q, g_k, conv_w, conv_b, dt_bias, a_log, d_skip, g_ssm,
           w_pa, w_pb, w_o, w_router, b_router, w_gate, w_up, w_down):
    depth = w_in.shape[0]
    bp, seq, d = x_prompt.shape
    db = x_sample.shape[0]
    assert x_sample.shape[1] == 1

    mod = _modulation(jnp.concatenate([c_prompt, c_sample], axis=0), w_mod, b_mod)
    yp = x_prompt
    ys = x_sample.reshape(1, db, d)
    outs = [[] for _ in range(9)]
    st_s = kv_all = None

    cache_kt = jnp.transpose(cache_k, (0, 1, 3, 4, 2))
    cache_vt = jnp.transpose(cache_v, (0, 1, 3, 4, 2))
    cache_lft = jnp.transpose(cache_logf, (0, 1, 3, 2))
    conv_state = jnp.transpose(state_conv, (0, 2, 1, 3))
    ssm_state = state_ssm.reshape(depth, db, D_INNER, D_STATE)
    bf = lambda w: w.astype(BF16)
    w_pa_b, w_pb_b, w_o_b, w_gate_b, w_up_b, w_down_b = bf(w_pa), bf(w_pb), bf(w_o), bf(w_gate), bf(w_up), bf(w_down)

    for l in range(depth):
        mp = [mod[l, :bp, i * d:(i + 1) * d][:, None, :] for i in range(6)]
        ms = [mod[l, bp:, i * d:(i + 1) * d][None] for i in range(6)]
        w_main, w_small = _split_w_in(w_in[l])

        qkv, z, xbc, gates, small = _norm_proj(
            yp, mp[1], mp[0], g_norm1[l], w_main.astype(BF16), w_small.astype(BF16), _sections(BF16),
            tm=min(TM_PROJ, seq), tn=IN_TILE, prec=None)
        qh, kh, vh, k_all, v_all, lf = _qkv_post_prompt(l, depth, kv_all, qkv, small, g_q[l], g_k[l], b_f[l],
                                                        tm=min(TM_POST, seq))
        kv_all = (k_all, v_all)
        cf, cft = _cumsum_logf(lf)
        o_att = _attention_prompt(qh, kh, vh, cf, cft, tq=min(TQ_ATTN, seq))
        y_ssm, st_p, cv_p = _ssd_prompt(xbc, z, small, conv_w[l], conv_b[l], dt_bias[l], a_log[l], d_skip[l],
                                        g_ssm[l])
        x1, h2, comb = _merge(l, o_att, y_ssm, gates, yp, mp[2], mp[4], mp[3], g_norm2[l],
                              w_pa_b, w_pb_b, w_o_b, w_router, b_router,
                              tm=min(TM_MERGE, seq), h2_dtype=BF16, prec=None)
        yp = _moe(l, h2, comb, x1, mp[5], w_gate_b, w_up_b, w_down_b, tm=min(TM_MOE, seq), prec=None)
        outs[2].append(lf)
        outs[3].append(st_p)
        outs[4].append(cv_p)

        qkv, z, xbc, gates, small = _norm_proj(
            ys, ms[1], ms[0], g_norm1[l], w_main, w_small, _sections(F32), tm=db, tn=IN_TILE, prec=HI)
        qn, k_new, v_new, lf_new = _qkv_post_sample(qkv, small, g_q[l], g_k[l], b_f[l])
        o_att_t = _attention_decode(l, page_table, qn.reshape(db, N_HEADS, HEAD_DIM),
                                    k_new.reshape(db, N_KV_HEADS, HEAD_DIM), v_new.reshape(db, N_KV_HEADS, HEAD_DIM),
                                    lf_new[0], cache_kt, cache_vt, cache_lft)
        o_att = jnp.swapaxes(o_att_t, 1, 2).reshape(1, db, ATT_WIDTH)
        conv_new, xd3, da, bm, cm, dx = _ssm_step_prep(
            l, xbc[0], conv_state, small[0], conv_w[l], conv_b[l], dt_bias[l], a_log[l], d_skip[l])
        st_s, y_t = _ssm_step_state(l, ssm_state, st_s, xd3, da, bm, cm)
        y_ssm = _ssm_step_gate(y_t, dx, z[0], g_ssm[l])
        x1, h2, comb = _merge(l, o_att, y_ssm[None], gates, ys, ms[2], ms[4], ms[3],
                              g_norm2[l], w_pa, w_pb, w_o, w_router, b_router, tm=db, h2_dtype=F32, prec=HI)
        ys = _moe(l, h2, comb, x1, ms[5], w_gate, w_up, w_down, tm=db, prec=HI)
        outs[5].append(k_new.reshape(db, 1, N_KV_HEADS, HEAD_DIM))
        outs[6].append(v_new.reshape(db, 1, N_KV_HEADS, HEAD_DIM))
        outs[7].append(lf_new.reshape(db, 1, N_HEADS))
        outs[8].append(conv_new)

    stk = [jnp.stack(o) if o else None for o in outs]
    k_prompt, v_prompt = (a.reshape(depth, bp, seq, N_KV_HEADS, HEAD_DIM) for a in kv_all)
    return (yp, ys.reshape(db, 1, d), k_prompt, v_prompt, stk[2], stk[3], stk[4], stk[5], stk[6], stk[7],
            st_s.reshape(depth, db, SSM_HEADS, SSM_HEAD_DIM, D_STATE), jnp.transpose(stk[8], (0, 2, 1, 3)))
```
